```python
import math
import jax, jax.numpy as jnp
from jax import lax
import numpy as np

D_MODEL = 1024
BATCH = 32
SEQ = 2048
DEPTH = 2
DEC_BATCH = 16
DEC_SEQ = 32
PAST_LEN = 4096

CHUNK = 64
N_A = DEPTH // 2
N_B = DEPTH - N_A

RET_HEADS = 4
RET_DK = D_MODEL // RET_HEADS
RET_DV = D_MODEL // RET_HEADS
RET_QK_W = RET_HEADS * RET_DK
RET_W = RET_HEADS * RET_DV

DIFF_HEADS = 4
DIFF_DQK = D_MODEL // (2 * DIFF_HEADS)
DIFF_DV = 2 * DIFF_DQK
DIFF_QW = DIFF_HEADS * 2 * DIFF_DQK
DIFF_VW = DIFF_HEADS * DIFF_DV

MEM_LEN = 256
MEM_HEADS = 4
MEM_HD = D_MODEL // 8
MEM_W = MEM_HEADS * MEM_HD

MIX_W = RET_W + MEM_W
SPLIT_A = (RET_QK_W, 2 * RET_QK_W, 2 * RET_QK_W + RET_W, 2 * RET_QK_W + 2 * RET_W, 2 * RET_QK_W + 2 * RET_W + MEM_W)
IN_A = 2 * RET_QK_W + 2 * RET_W + 2 * MEM_W
SPLIT_B = (DIFF_QW, DIFF_QW + DIFF_VW, DIFF_QW + DIFF_VW + MEM_W)
IN_B = DIFF_QW + DIFF_VW + 2 * MEM_W

Q_BLOCK = 128
ROPE_BASE = 10000.0
EPS = 1e-6

kernel_name = "yoco_retention_diffattn_stream_step"


def rms_norm(x, gain=None, eps=EPS):
    xf = x.astype(jnp.float32)
    y = xf * lax.rsqrt(jnp.mean(jnp.square(xf), axis=-1, keepdims=True) + eps)
    if gain is not None:
        y = y * gain.astype(jnp.float32)
    return y.astype(x.dtype)


def rope(x, pos):
    half = x.shape[-1] // 2
    inv = ROPE_BASE ** (-jnp.arange(half, dtype=jnp.float32) / half)
    ang = pos.astype(jnp.float32)[:, None] * inv[None, :]
    cos = jnp.cos(ang)[None, :, None, :]
    sin = jnp.sin(ang)[None, :, None, :]
    x1, x2 = x[..., :half], x[..., half:]
    return jnp.concatenate([x1 * cos - x2 * sin, x2 * cos + x1 * sin], axis=-1)


def ret_log_decay():
    return jnp.log(1.0 - 2.0 ** (-5.0 - jnp.arange(RET_HEADS, dtype=jnp.float32)))


def retention_block(q, k, v, S, log_g):
    L = q.shape[1]
    idx = jnp.arange(L, dtype=jnp.float32)
    diff = idx[:, None] - idx[None, :]
    decay = jnp.where(diff >= 0, jnp.exp(log_g[:, None, None] * jnp.maximum(diff, 0.0)[None]), 0.0)
    scores = jnp.einsum('blhd,bmhd->bhlm', q, k) * decay[None]
    inner = jnp.einsum('bhlm,bmhe->blhe', scores, v)
    cross = jnp.einsum('blhd,bhde->blhe', q, S) * jnp.exp(log_g[None, :, None] * (idx + 1.0)[:, None, None])
    kw = k * jnp.exp(log_g[:, None] * (L - 1.0 - idx)[None, :]).T[None, :, :, None]
    S_new = jnp.exp(log_g * L)[None, :, None, None] * S + jnp.einsum('blhd,blhe->bhde', kw, v)
    return inner + cross, S_new


def retention_seq(q, k, v, S0, log_g):
    B, T = q.shape[:2]
    if T <= CHUNK:
        return retention_block(q, k, v, S0, log_g)
    nc = T // CHUNK

    def to_chunks(a):
        return jnp.swapaxes(a.reshape(B, nc, CHUNK, *a.shape[2:]), 0, 1)

    def step(S, qkv):
        o, S = retention_block(qkv[0], qkv[1], qkv[2], S, log_g)
        return S, o

    S, o = lax.scan(step, S0, (to_chunks(q), to_chunks(k), to_chunks(v)))
    return jnp.swapaxes(o, 0, 1).reshape(B, T, *o.shape[3:]), S


def mem_attention(q, mk, mv):
    s = jnp.einsum('bthd,bmhd->bhtm', q, mk).astype(jnp.float32) * (MEM_HD ** -0.5)
    p = jax.nn.softmax(s, axis=-1)
    return jnp.einsum('bhtm,bmhd->bthd', p.astype(mv.dtype), mv)


def memory_kv(mem, norm_mem, w_mem_kv):
    B, M = mem.shape[:2]
    base = rms_norm(mem)
    kv = jnp.einsum('bmd,ld,lde->lbme', base, norm_mem.astype(base.dtype), w_mem_kv)
    nl = kv.shape[0]
    mk = kv[..., :MEM_W].reshape(nl, B, M, MEM_HEADS, MEM_HD)
    mv = kv[..., MEM_W:].reshape(nl, B, M, MEM_HEADS, MEM_HD)
    return mk, mv


def retention_layer(x, pos, S0, mk, mv, norm_g, w_in, w_out):
    B, T, _ = x.shape
    h = rms_norm(x, norm_g)
    q, k, v, g, qm, gm = jnp.split(h @ w_in, SPLIT_A, axis=-1)
    q = rope(q.astype(jnp.float32).reshape(B, T, RET_HEADS, RET_DK), pos)
    k = rope(k.astype(jnp.float32).reshape(B, T, RET_HEADS, RET_DK), pos) * (RET_DK ** -0.5)
    v = v.astype(jnp.float32).reshape(B, T, RET_HEADS, RET_DV)
    o, S = retention_seq(q, k, v, S0.astype(jnp.float32), ret_log_decay())
    o = rms_norm(o).reshape(B, T, RET_W).astype(x.dtype) * jax.nn.silu(g)
    m = mem_attention(qm.reshape(B, T, MEM_HEADS, MEM_HD), mk, mv).reshape(B, T, MEM_W).astype(x.dtype) * jax.nn.silu(gm)
    return x + jnp.concatenate([o, m], axis=-1) @ w_out, S


def diff_attend(q, k, v, lam, mask):
    s = jnp.einsum('bqhcd,bkhcd->bhcqk', q, k).astype(jnp.float32) * (DIFF_DQK ** -0.5)
    if mask is not None:
        s = jnp.where(mask, s, -jnp.inf)
    p = jax.nn.softmax(s, axis=-1)
    a = p[:, :, 0] - lam * p[:, :, 1]
    return jnp.einsum('bhqk,bkhe->bqhe', a.astype(v.dtype), v)


def diff_attend_blocks(q, k, v, lam):
    B, T = q.shape[:2]
    nb = T // Q_BLOCK
    qb = jnp.moveaxis(q.reshape(B, nb, Q_BLOCK, *q.shape[2:]), 1, 0)
    key_chunk = jnp.arange(T) // CHUNK

    def block(args):
        qi, i = args
        q_chunk = (i * Q_BLOCK + jnp.arange(Q_BLOCK)) // CHUNK
        return diff_attend(qi, k, v, lam, key_chunk[None, :] <= q_chunk[:, None])

    o = lax.map(block, (qb, jnp.arange(nb)))
    return jnp.moveaxis(o, 0, 1).reshape(B, T, *o.shape[3:])


def diff_layer(x, k_all, v_all, mk, mv, norm_g, w_in, lam_p, subln_g, w_out, lam_init, blocked):
    B, T, _ = x.shape
    h = rms_norm(x, norm_g)
    q, g, qm, gm = jnp.split(h @ w_in, SPLIT_B, axis=-1)
    q = q.reshape(B, T, DIFF_HEADS, 2, DIFF_DQK)
    lp = lam_p.astype(jnp.float32)
    lam = jnp.exp(jnp.sum(lp[0] * lp[1])) - jnp.exp(jnp.sum(lp[2] * lp[3])) + lam_init
    if blocked:
        o = diff_attend_blocks(q, k_all, v_all, lam)
    else:
        o = diff_attend(q, k_all, v_all, lam, None)
    o = rms_norm(o, subln_g) * (1.0 - lam_init)
    o = o.reshape(B, T, DIFF_VW).astype(x.dtype) * jax.nn.silu(g)
    m = mem_attention(qm.reshape(B, T, MEM_HEADS, MEM_HD), mk, mv).reshape(B, T, MEM_W).astype(x.dtype) * jax.nn.silu(gm)
    return x + jnp.concatenate([o, m], axis=-1) @ w_out


def shared_kv(h, norm_kv, w_kv):
    B, T, _ = h.shape
    kv = rms_norm(h, norm_kv) @ w_kv
    k = kv[..., :DIFF_QW].reshape(B, T, DIFF_HEADS, 2, DIFF_DQK)
    v = kv[..., DIFF_QW:].reshape(B, T, DIFF_HEADS, DIFF_DV)
    return k, v


def trunk(x, pos, S_in, mem_k, mem_v, past_k, past_v, norm_a, w_in_a, w_out_a, norm_kv, w_kv,
          norm_b, w_in_b, diff_lambda, subln_b, w_out_b, norm_f, blocked):
    h = x
    S_out = []
    for l in range(N_A):
        h, S = retention_layer(h, pos, S_in[l], mem_k[l], mem_v[l], norm_a[l], w_in_a[l], w_out_a[l])
        S_out.append(S.astype(x.dtype))
    k_new, v_new = shared_kv(h, norm_kv, w_kv)
    if past_k is None:
        k_all, v_all = k_new, v_new
    else:
        k_all = jnp.concatenate([past_k.astype(k_new.dtype), k_new], axis=1)
        v_all = jnp.concatenate([past_v.astype(v_new.dtype), v_new], axis=1)
    for j in range(N_B):
        l = N_A + j
        lam_init = 0.8 - 0.6 * math.exp(-0.3 * l)
        h = diff_layer(h, k_all, v_all, mem_k[l], mem_v[l], norm_b[j], w_in_b[j], diff_lambda[j],
                       subln_b[j], w_out_b[j], lam_init, blocked)
    return rms_norm(h, norm_f), jnp.stack(S_out), k_new, v_new


def setup_inputs(seed: int = 0) -> dict:
    key = jax.random.key(seed)
    ks = jax.random.split(key, 21)
    f32 = jnp.float32

    def nrm(k, shape, s):
        return jax.random.normal(k, shape, f32) * s

    def gain(k, shape):
        return 1.0 + 0.02 * jax.random.normal(k, shape, f32)

    return {
        "x_prompt": nrm(ks[0], (BATCH, SEQ, D_MODEL), 1.0),
        "x_sample": nrm(ks[1], (DEC_BATCH, DEC_SEQ, D_MODEL), 1.0),
        "state_ret": nrm(ks[2], (N_A, DEC_BATCH, RET_HEADS, RET_DK, RET_DV), 0.5),
        "cache_k": nrm(ks[3], (DEC_BATCH, PAST_LEN, DIFF_HEADS, 2, DIFF_DQK), 1.0),
        "cache_v": nrm(ks[4], (DEC_BATCH, PAST_LEN, DIFF_HEADS, DIFF_DV), 1.0),
        "cache_mem_k": nrm(ks[5], (DEPTH, DEC_BATCH, MEM_LEN, MEM_HEADS, MEM_HD), 1.0),
        "cache_mem_v": nrm(ks[6], (DEPTH, DEC_BATCH, MEM_LEN, MEM_HEADS, MEM_HD), 1.0),
        "mem_prompt": nrm(ks[7], (BATCH, MEM_LEN, D_MODEL), 1.0),
        "norm_a": gain(ks[8], (N_A, D_MODEL)),
        "w_in_a": nrm(ks[9], (N_A, D_MODEL, IN_A), D_MODEL ** -0.5),
        "w_out_a": nrm(ks[10], (N_A, MIX_W, D_MODEL), MIX_W ** -0.5),
        "norm_kv": gain(ks[11], (D_MODEL,)),
        "w_kv": nrm(ks[12], (D_MODEL, DIFF_QW + DIFF_VW), D_MODEL ** -0.5),
        "norm_b": gain(ks[13], (N_B, D_MODEL)),
        "w_in_b": nrm(ks[14], (N_B, D_MODEL, IN_B), D_MODEL ** -0.5),
        "diff_lambda": nrm(ks[15], (N_B, 4, DIFF_DQK), 0.1),
        "subln_b": gain(ks[16], (N_B, DIFF_DV)),
        "w_out_b": nrm(ks[17], (N_B, MIX_W, D_MODEL), MIX_W ** -0.5),
        "norm_mem": gain(ks[18], (DEPTH, D_MODEL)),
        "w_mem_kv": nrm(ks[19], (DEPTH, D_MODEL, 2 * MEM_W), D_MODEL ** -0.5),
        "norm_f": gain(ks[20], (D_MODEL,)),
    }


def reference(x_prompt, x_sample, state_ret, cache_k, cache_v, cache_mem_k, cache_mem_v, mem_prompt,
              norm_a, w_in_a, w_out_a, norm_kv, w_kv, norm_b, w_in_b, diff_lambda, subln_b, w_out_b,
              norm_mem, w_mem_kv, norm_f):
    Bp, Tp = x_prompt.shape[:2]
    mem_k_prompt, mem_v_prompt = memory_kv(mem_prompt, norm_mem, w_mem_kv)
    S0 = jnp.zeros((N_A, Bp, RET_HEADS, RET_DK, RET_DV), jnp.float32)
    y_prompt, state_ret_prompt, k_prompt, v_prompt = trunk(
        x_prompt, jnp.arange(Tp), S0, mem_k_prompt, mem_v_prompt, None, None,
        norm_a, w_in_a, w_out_a, norm_kv, w_kv, norm_b, w_in_b, diff_lambda, subln_b, w_out_b, norm_f, True)
    Ts = x_sample.shape[1]
    y_sample, state_ret_sample, k_sample, v_sample = trunk(
        x_sample, PAST_LEN + jnp.arange(Ts), state_ret, cache_mem_k, cache_mem_v, cache_k, cache_v,
        norm_a, w_in_a, w_out_a, norm_kv, w_kv, norm_b, w_in_b, diff_lambda, subln_b, w_out_b, norm_f, False)
    return (y_prompt, y_sample, state_ret_prompt, k_prompt, v_prompt, mem_k_prompt, mem_v_prompt,
            state_ret_sample, k_sample, v_sample)
```

```python
import functools
import math

import jax
import jax.numpy as jnp
from jax import lax
from jax.experimental import pallas as pl
from jax.experimental.pallas import tpu as pltpu

F32 = jnp.float32
BF16 = jnp.bfloat16

EPS = 1e-6
ROPE_BASE = 10000.0
CHUNK = 64
N_HEADS = 4
HEAD_W = 256
DQK = 128
MEM_HD = 128
MEM_W = N_HEADS * MEM_HD
NEG_BIG = -1e30

VMEM_LIMIT_BYTES = 56 * 1024 * 1024


def _dot(a, b):
    return jnp.dot(a, b, preferred_element_type=F32)


def _dot_nt(a, b):
    return lax.dot_general(a, b, (((1,), (1,)), ((), ())), preferred_element_type=F32)


def _dot_tn(a, b):
    return lax.dot_general(a, b, (((0,), (0,)), ((), ())), preferred_element_type=F32)


def _rms(x):
    return x * lax.rsqrt(jnp.mean(x * x, axis=-1, keepdims=True) + EPS)


def _silu(g):
    return g / (1.0 + jnp.exp(-g))


def _const_spec(shape):
    nd = len(shape)
    return pl.BlockSpec(shape, lambda *_: (0,) * nd, pipeline_mode=pl.Buffered(1))


def _mem_attention(qm, gm, mk_ref, mv_ref, nb, rows, cat_scr, col0):
    for hh in range(N_HEADS):
        cs = slice(hh * MEM_HD, (hh + 1) * MEM_HD)
        q = (qm[rows, cs] * (MEM_HD ** -0.5)).astype(BF16)
        s = _dot_nt(q, mk_ref[nb, :, cs])
        p = jnp.exp(s - jnp.max(s, axis=-1, keepdims=True))
        l = jnp.sum(p, axis=-1, keepdims=True)
        o = _dot(p.astype(BF16), mv_ref[nb, :, cs]) / l
        cat_scr[rows, col0 + hh * MEM_HD: col0 + (hh + 1) * MEM_HD] = (o * _silu(gm[rows, cs])).astype(BF16)


def _memkv_kernel(mem_ref, g_ref, w_ref, mk_ref, mv_ref, mkb_ref, mvb_ref):
    hb = (_rms(mem_ref[...]) * g_ref[0]).astype(BF16)
    kv = _dot(hb, w_ref[0])
    mk, mv = kv[:, :MEM_W], kv[:, MEM_W:]
    mk_ref[0] = mk
    mv_ref[0] = mv
    mkb_ref[0] = mk.astype(BF16)
    mvb_ref[0] = mv.astype(BF16)


def _memory_kv(mem, norm_mem, w_mem_kv):
    B, M, D = mem.shape
    nl = norm_mem.shape[0]
    rows = B * M
    tm = 512 if rows % 512 == 0 else M
    f = pl.pallas_call(
        _memkv_kernel,
        grid=(nl, rows // tm),
        in_specs=[
            pl.BlockSpec((tm, D), lambda l, i: (i, 0)),
            pl.BlockSpec((1, 1, D), lambda l, i: (l, 0, 0)),
            pl.BlockSpec((1, D, 2 * MEM_W), lambda l, i: (l, 0, 0)),
        ],
        out_specs=[pl.BlockSpec((1, tm, MEM_W), lambda l, i: (l, i, 0))] * 4,
        out_shape=[jax.ShapeDtypeStruct((nl, rows, MEM_W), F32)] * 2
        + [jax.ShapeDtypeStruct((nl, rows, MEM_W), BF16)] * 2,
        compiler_params=pltpu.CompilerParams(
            dimension_semantics=("arbitrary", "arbitrary"), vmem_limit_bytes=VMEM_LIMIT_BYTES),
        name="memkv",
    )
    mk, mv, mkb, mvb = f(mem.reshape(rows, D), norm_mem.reshape(nl, 1, D), w_mem_kv.astype(BF16))
    shp5 = (nl, B, M, N_HEADS, MEM_HD)
    return mk.reshape(shp5), mv.reshape(shp5), mkb.reshape(nl, B, M, MEM_W), mvb.reshape(nl, B, M, MEM_W)


def _layer_a_kernel(*refs, NB, TT, L, has_state):
    it = iter(refs)
    x_ref, cos_ref, sin_ref = next(it), next(it), next(it)
    s0_ref = next(it) if has_state else None
    mk_ref, mv_ref, na_ref, win_ref, wout_ref, nkv_ref, wkv_ref = (next(it) for _ in range(7))
    h1_ref, k_ref, v_ref, kb_ref, vb_ref, S_ref = (next(it) for _ in range(6))
    h_scr, cat_scr, dec_scr = (next(it) for _ in range(3))

    R = NB * TT
    D = x_ref.shape[-1]
    QW = N_HEADS * HEAD_W
    t = pl.program_id(1)
    log_g = [math.log(1.0 - 2.0 ** (-5.0 - h)) for h in range(N_HEADS)]

    @pl.when((pl.program_id(0) == 0) & (t == 0))
    def _():
        d = (lax.broadcasted_iota(jnp.int32, (L, L), 0) - lax.broadcasted_iota(jnp.int32, (L, L), 1)).astype(F32)
        for h in range(N_HEADS):
            dec_scr[h] = jnp.where(d >= 0, jnp.exp(log_g[h] * jnp.maximum(d, 0.0)), 0.0)

    @pl.when(t == 0)
    def _():
        if has_state:
            S_ref[...] = s0_ref[...]
        else:
            S_ref[...] = jnp.zeros(S_ref.shape, F32)

    x = x_ref[...].reshape(R, D)
    h_scr[...] = (_rms(x) * na_ref[...]).astype(BF16)
    cos = cos_ref[...]
    sin = sin_ref[...]
    half = HEAD_W // 2
    idx = lax.broadcasted_iota(jnp.int32, (L, 1), 0).astype(F32)

    def rope(c0):
        y = _dot(h_scr[...], win_ref[:, c0:c0 + HEAD_W])
        y1, y2 = y[:, :half], y[:, half:]
        return jnp.concatenate([y1 * cos - y2 * sin, y2 * cos + y1 * sin], axis=-1)

    for h in range(N_HEADS):
        q = rope(h * HEAD_W).astype(BF16)
        k = rope(QW + h * HEAD_W) * (HEAD_W ** -0.5)
        v = _dot(h_scr[...], win_ref[:, 2 * QW + h * HEAD_W: 2 * QW + (h + 1) * HEAD_W]).astype(BF16)
        g = _dot(h_scr[...], win_ref[:, 3 * QW + h * HEAD_W: 3 * QW + (h + 1) * HEAD_W])
        a_in = jnp.exp(log_g[h] * (idx + 1.0))
        b_out = jnp.exp(log_g[h] * (L - 1.0 - idx))
        g_blk = math.exp(log_g[h] * L)
        for nb in range(NB):
            for j in range(TT // L):
                rows = slice(nb * TT + j * L, nb * TT + (j + 1) * L)
                qb, vb = q[rows], v[rows]
                kf = k[rows]
                S = S_ref[nb, h]
                sc = _dot_nt(qb, kf.astype(BF16)) * dec_scr[h]
                o = _dot(sc.astype(BF16), vb) + _dot(qb, S.astype(BF16)) * a_in
                S_ref[nb, h] = g_blk * S + _dot_tn((kf * b_out).astype(BF16), vb)
                cat_scr[rows, h * HEAD_W:(h + 1) * HEAD_W] = (_rms(o) * _silu(g[rows])).astype(BF16)

    qm = _dot(h_scr[...], win_ref[:, 4 * QW: 4 * QW + MEM_W])
    gm = _dot(h_scr[...], win_ref[:, 4 * QW + MEM_W: 4 * QW + 2 * MEM_W])
    for nb in range(NB):
        _mem_attention(qm, gm, mk_ref, mv_ref, nb, slice(nb * TT, (nb + 1) * TT), cat_scr, QW)

    NC = 512
    for c in range(D // NC):
        cs = slice(c * NC, (c + 1) * NC)
        y = _dot(cat_scr[...], wout_ref[:, cs]) + x_ref[:, :, cs].reshape(R, NC)
        h1_ref[:, :, cs] = y.reshape(NB, TT, NC)

    hk = (_rms(h1_ref[...].reshape(R, D)) * nkv_ref[...]).astype(BF16)
    KW = k_ref.shape[-1]
    for c in range(2 * KW // NC):
        kv = _dot(hk, wkv_ref[:, c * NC:(c + 1) * NC]).reshape(NB, TT, NC)
        if c < KW // NC:
            cs = slice(c * NC, (c + 1) * NC)
            k_ref[:, :, cs] = kv
            kb_ref[:, :, cs] = kv.astype(BF16)
        else:
            cs = slice(c * NC - KW, (c + 1) * NC - KW)
            v_ref[:, :, cs] = kv
            vb_ref[:, :, cs] = kv.astype(BF16)


def _layer_a(x, cos, sin, s0, mkb, mvb, norm_a, w_in, w_out, norm_kv, w_kv, *, NB, TT, L):
    B, T, D = x.shape
    R = NB * TT
    KW = w_kv.shape[1] // 2
    M = mkb.shape[1]
    has_state = s0 is not None
    bt = lambda b, t: (b, t, 0)
    b0 = lambda b, t: (b, 0, 0)
    in_specs = [pl.BlockSpec((NB, TT, D), bt),
                pl.BlockSpec((R, DQK), lambda b, t: (t, 0)),
                pl.BlockSpec((R, DQK), lambda b, t: (t, 0))]
    args = [x, cos, sin]
    if has_state:
        in_specs.append(pl.BlockSpec((NB, N_HEADS, HEAD_W, HEAD_W), lambda b, t: (b, 0, 0, 0)))
        args.append(s0)
    in_specs += [pl.BlockSpec((NB, M, MEM_W), b0), pl.BlockSpec((NB, M, MEM_W), b0),
                 _const_spec((1, D)), _const_spec(w_in.shape), _const_spec(w_out.shape),
                 _const_spec((1, D)), _const_spec(w_kv.shape)]
    args += [mkb, mvb, norm_a.reshape(1, D), w_in, w_out, norm_kv.reshape(1, D), w_kv]
    out_specs = [pl.BlockSpec((NB, TT, D), bt)] + [pl.BlockSpec((NB, TT, KW), bt)] * 4 + [
        pl.BlockSpec((NB, N_HEADS, HEAD_W, HEAD_W), lambda b, t: (b, 0, 0, 0))]
    out_shape = [jax.ShapeDtypeStruct((B, T, D), F32)] + [jax.ShapeDtypeStruct((B, T, KW), F32)] * 2 + [
        jax.ShapeDtypeStruct((B, T, KW), BF16)] * 2 + [jax.ShapeDtypeStruct((B, N_HEADS, HEAD_W, HEAD_W), F32)]
    f = pl.pallas_call(
        functools.partial(_layer_a_kernel, NB=NB, TT=TT, L=L, has_state=has_state),
        grid=(B // NB, T // TT),
        in_specs=in_specs,
        out_specs=out_specs,
        out_shape=out_shape,
        scratch_shapes=[pltpu.VMEM((R, D), BF16), pltpu.VMEM((R, N_HEADS * HEAD_W + MEM_W), BF16),
                        pltpu.VMEM((N_HEADS, L, L), F32)],
        compiler_params=pltpu.CompilerParams(
            dimension_semantics=("arbitrary", "arbitrary"), vmem_limit_bytes=VMEM_LIMIT_BYTES),
        name="layer_a",
    )
    return f(*args)


def _diff_lambda(lp_ref, lam_init):
    lp = lp_ref[...]
    a = jnp.sum(lp[0:1] * lp[1:2], axis=-1, keepdims=True)
    b = jnp.sum(lp[2:3] * lp[3:4], axis=-1, keepdims=True)
    return jnp.exp(a) - jnp.exp(b) + lam_init


def _diff_project_q(h1, nb_ref, win_ref, h_scr, q_scr):
    h_scr[...] = (_rms(h1) * nb_ref[...]).astype(BF16)
    for hh in range(N_HEADS):
        q = _dot(h_scr[...], win_ref[:, hh * HEAD_W:(hh + 1) * HEAD_W]) * (DQK ** -0.5)
        q_scr[hh] = q.astype(BF16)


def _diff_init(m_scr, l_scr, acc_scr):
    m_scr[...] = jnp.full(m_scr.shape, NEG_BIG, F32)
    l_scr[...] = jnp.zeros(l_scr.shape, F32)
    acc_scr[...] = jnp.zeros(acc_scr.shape, F32)


def _diff_tile(q_scr, kt_of, vt_of, mask, m_scr, l_scr, acc_scr):
    TQ = q_scr.shape[1]
    for hh in range(N_HEADS):
        kt, vt = kt_of(hh), vt_of(hh)
        ps, alphas = [], []
        for c in range(2):
            i = 2 * hh + c
            s = _dot_nt(q_scr[hh, :, c * DQK:(c + 1) * DQK], kt[:, c * DQK:(c + 1) * DQK])
            if mask is not None:
                s = jnp.where(mask, s, NEG_BIG)
            m_prev = m_scr[i]
            m_new = jnp.maximum(m_prev, jnp.max(s, axis=-1, keepdims=True))
            alpha = jnp.exp(m_prev - m_new)
            p = jnp.exp(s - m_new)
            l_scr[i] = alpha * l_scr[i] + jnp.sum(p, axis=-1, keepdims=True)
            m_scr[i] = m_new
            ps.append(p.astype(BF16))
            alphas.append(alpha)
        pv = _dot(jnp.concatenate(ps, axis=0), vt)
        acc_scr[2 * hh] = alphas[0] * acc_scr[2 * hh] + pv[:TQ]
        acc_scr[2 * hh + 1] = alphas[1] * acc_scr[2 * hh + 1] + pv[TQ:]


def _diff_finish(h1, lam, lam_init, l_scr, acc_scr, h_scr, win_ref, sub_ref, mk_ref, mv_ref, wout_ref,
                 nf_ref, cat_scr, y_ref):
    TQ, D = h1.shape
    QW = N_HEADS * HEAD_W
    for hh in range(N_HEADS):
        o = acc_scr[2 * hh] / l_scr[2 * hh] - lam * (acc_scr[2 * hh + 1] / l_scr[2 * hh + 1])
        o = _rms(o) * sub_ref[...] * (1.0 - lam_init)
        g = _dot(h_scr[...], win_ref[:, QW + hh * HEAD_W: QW + (hh + 1) * HEAD_W])
        cat_scr[:, hh * HEAD_W:(hh + 1) * HEAD_W] = (o * _silu(g)).astype(BF16)
    qm = _dot(h_scr[...], win_ref[:, 2 * QW: 2 * QW + MEM_W])
    gm = _dot(h_scr[...], win_ref[:, 2 * QW + MEM_W: 2 * QW + 2 * MEM_W])
    _mem_attention(qm, gm, mk_ref, mv_ref, 0, slice(0, TQ), cat_scr, QW)
    NC = 512
    for c in range(D // NC):
        cs = slice(c * NC, (c + 1) * NC)
        y_ref[0, :, cs] = _dot(cat_scr[...], wout_ref[:, cs]) + h1[:, cs]
    y_ref[0] = _rms(y_ref[0]) * nf_ref[...]


def _layer_b_prompt_kernel(h1_ref, kb_ref, vb_ref, mk_ref, mv_ref, nb_ref, win_ref, lp_ref, sub_ref, wout_ref,
                           nf_ref, y_ref, h_scr, q_scr, m_scr, l_scr, acc_scr, cat_scr, *, TQ, lam_init):
    qi = pl.program_id(1)
    h1 = h1_ref[0]
    _diff_project_q(h1, nb_ref, win_ref, h_scr, q_scr)
    _diff_init(m_scr, l_scr, acc_scr)

    def tile(j, mask):
        k0 = pl.multiple_of(j * TQ, TQ)
        kt_of = lambda hh: kb_ref[0, pl.ds(k0, TQ), hh * HEAD_W:(hh + 1) * HEAD_W]
        vt_of = lambda hh: vb_ref[0, pl.ds(k0, TQ), hh * HEAD_W:(hh + 1) * HEAD_W]
        _diff_tile(q_scr, kt_of, vt_of, mask, m_scr, l_scr, acc_scr)

    def body(j, carry):
        tile(j, None)
        return carry

    lax.fori_loop(0, qi, body, 0)
    r = lax.broadcasted_iota(jnp.int32, (TQ, TQ), 0) // CHUNK
    c = lax.broadcasted_iota(jnp.int32, (TQ, TQ), 1) // CHUNK
    tile(qi, c <= r)
    lam = _diff_lambda(lp_ref, lam_init)
    _diff_finish(h1, lam, lam_init, l_scr, acc_scr, h_scr, win_ref, sub_ref, mk_ref, mv_ref, wout_ref, nf_ref,
                 cat_scr, y_ref)


def _layer_b_prompt(h1, kb, vb, mkb, mvb, norm_b, w_in, lam_p, subln, w_out, norm_f, *, TQ, lam_init):
    B, T, D = h1.shape
    KW = kb.shape[-1]
    M = mkb.shape[1]
    b0 = lambda b, q: (b, 0, 0)
    f = pl.pallas_call(
        functools.partial(_layer_b_prompt_kernel, TQ=TQ, lam_init=lam_init),
        grid=(B, T // TQ),
        in_specs=[pl.BlockSpec((1, TQ, D), lambda b, q: (b, q, 0)),
                  pl.BlockSpec((1, T, KW), b0), pl.BlockSpec((1, T, KW), b0),
                  pl.BlockSpec((1, M, MEM_W), b0), pl.BlockSpec((1, M, MEM_W), b0),
                  _const_spec((1, D)), _const_spec(w_in.shape), _const_spec(lam_p.shape),
                  _const_spec((1, HEAD_W)), _const_spec(w_out.shape), _const_spec((1, D))],
        out_specs=pl.BlockSpec((1, TQ, D), lambda b, q: (b, q, 0)),
        out_shape=jax.ShapeDtypeStruct((B, T, D), F32),
        scratch_shapes=[pltpu.VMEM((TQ, D), BF16), pltpu.VMEM((N_HEADS, TQ, HEAD_W), BF16),
                        pltpu.VMEM((2 * N_HEADS, TQ, 1), F32), pltpu.VMEM((2 * N_HEADS, TQ, 1), F32),
                        pltpu.VMEM((2 * N_HEADS, TQ, HEAD_W), F32),
                        pltpu.VMEM((TQ, N_HEADS * HEAD_W + MEM_W), BF16)],
        compiler_params=pltpu.CompilerParams(
            dimension_semantics=("arbitrary", "arbitrary"), vmem_limit_bytes=VMEM_LIMIT_BYTES),
        name="layer_b_prompt",
    )
    return f(h1, kb, vb, mkb, mvb, norm_b.reshape(1, D), w_in, lam_p, subln.reshape(1, HEAD_W), w_out,
             norm_f.reshape(1, D))


def _layer_b_sample_kernel(h1_ref, ck_ref, cv_ref, kn_ref, vn_ref, mk_ref, mv_ref, nb_ref, win_ref, lp_ref, sub_ref,
                           wout_ref, nf_ref, y_ref, h_scr, q_scr, m_scr, l_scr, acc_scr, cat_scr, *, lam_init):
    kt = pl.program_id(1)

    @pl.when(kt == 0)
    def _():
        _diff_project_q(h1_ref[0], nb_ref, win_ref, h_scr, q_scr)
        _diff_init(m_scr, l_scr, acc_scr)

    _diff_tile(q_scr,
               lambda hh: ck_ref[0, :, hh * HEAD_W:(hh + 1) * HEAD_W].astype(BF16),
               lambda hh: cv_ref[0, :, hh * HEAD_W:(hh + 1) * HEAD_W].astype(BF16),
               None, m_scr, l_scr, acc_scr)

    @pl.when(kt == pl.num_programs(1) - 1)
    def _():
        _diff_tile(q_scr,
                   lambda hh: kn_ref[0, :, hh * HEAD_W:(hh + 1) * HEAD_W],
                   lambda hh: vn_ref[0, :, hh * HEAD_W:(hh + 1) * HEAD_W],
                   None, m_scr, l_scr, acc_scr)
        lam = _diff_lambda(lp_ref, lam_init)
        _diff_finish(h1_ref[0], lam, lam_init, l_scr, acc_scr, h_scr, win_ref, sub_ref, mk_ref, mv_ref, wout_ref,
                     nf_ref, cat_scr, y_ref)


def _layer_b_sample(h1, cache_k, cache_v, kb, vb, mkb, mvb, norm_b, w_in, lam_p, subln, w_out, norm_f, *, TK,
                    lam_init):
    B, T, D = h1.shape
    P, KW = cache_k.shape[1], cache_k.shape[2]
    M = mkb.shape[1]
    b0 = lambda b, k: (b, 0, 0)
    f = pl.pallas_call(
        functools.partial(_layer_b_sample_kernel, lam_init=lam_init),
        grid=(B, P // TK),
        in_specs=[pl.BlockSpec((1, T, D), b0),
                  pl.BlockSpec((1, TK, KW), lambda b, k: (b, k, 0)),
                  pl.BlockSpec((1, TK, KW), lambda b, k: (b, k, 0)),
                  pl.BlockSpec((1, T, KW), b0), pl.BlockSpec((1, T, KW), b0),
                  pl.BlockSpec((1, M, MEM_W), b0), pl.BlockSpec((1, M, MEM_W), b0),
                  _const_spec((1, D)), _const_spec(w_in.shape), _const_spec(lam_p.shape),
                  _const_spec((1, HEAD_W)), _const_spec(w_out.shape), _const_spec((1, D))],
        out_specs=pl.BlockSpec((1, T, D), b0),
        out_shape=jax.ShapeDtypeStruct((B, T, D), F32),
        scratch_shapes=[pltpu.VMEM((T, D), BF16), pltpu.VMEM((N_HEADS, T, HEAD_W), BF16),
                        pltpu.VMEM((2 * N_HEADS, T, 1), F32), pltpu.VMEM((2 * N_HEADS, T, 1), F32),
                        pltpu.VMEM((2 * N_HEADS, T, HEAD_W), F32),
                        pltpu.VMEM((T, N_HEADS * HEAD_W + MEM_W), BF16)],
        compiler_params=pltpu.CompilerParams(
            dimension_semantics=("arbitrary", "arbitrary"), vmem_limit_bytes=VMEM_LIMIT_BYTES),
        name="layer_b_sample",
    )
    return f(h1, cache_k, cache_v, kb, vb, mkb, mvb, norm_b.reshape(1, D), w_in, lam_p, subln.reshape(1, HEAD_W),
             w_out, norm_f.reshape(1, D))


def _rope_tables(pos):
    half = DQK
    inv = ROPE_BASE ** (-jnp.arange(half, dtype=F32) / half)
    ang = pos.astype(F32)[:, None] * inv[None, :]
    return jnp.cos(ang), jnp.sin(ang)


def _pick(n, candidates):
    for c in candidates:
        if n % c == 0:
            return c
    return n


def kernel(x_prompt, x_sample, state_ret, cache_k, cache_v, cache_mem_k, cache_mem_v, mem_prompt,
           norm_a, w_in_a, w_out_a, norm_kv, w_kv, norm_b, w_in_b, diff_lambda, subln_b, w_out_b,
           norm_mem, w_mem_kv, norm_f):
    depth = norm_mem.shape[0]
    assert norm_a.shape[0] == 1 and norm_b.shape[0] == 1 and depth == 2, "kernel is written for depth 2"
    lam_init = 0.8 - 0.6 * math.exp(-0.3 * 1)
    wia, woa, wkv = w_in_a[0].astype(BF16), w_out_a[0].astype(BF16), w_kv.astype(BF16)
    wib, wob = w_in_b[0].astype(BF16), w_out_b[0].astype(BF16)

    Bp, Tp, D = x_prompt.shape
    mk_p, mv_p, mkb_p, mvb_p = _memory_kv(mem_prompt, norm_mem, w_mem_kv)
    cos_p, sin_p = _rope_tables(jnp.arange(Tp))
    tt = _pick(Tp, (256,))
    h1_p, k_p, v_p, kb_p, vb_p, S_p = _layer_a(
        x_prompt, cos_p, sin_p, None, mkb_p[0], mvb_p[0], norm_a[0], wia, woa, norm_kv, wkv, NB=1, TT=tt, L=tt)
    y_p = _layer_b_prompt(h1_p, kb_p, vb_p, mkb_p[1], mvb_p[1], norm_b[0], wib, diff_lambda[0], subln_b[0], wob,
                          norm_f, TQ=tt, lam_init=lam_init)

    Bs, Ts, _ = x_sample.shape
    past = cache_k.shape[1]
    nbs = _pick(Bs, (4, 2, 1))
    cos_s, sin_s = _rope_tables(past + jnp.arange(Ts))
    cos_s, sin_s = jnp.tile(cos_s, (nbs, 1)), jnp.tile(sin_s, (nbs, 1))
    M = cache_mem_k.shape[2]
    cmk = cache_mem_k.reshape(depth, Bs, M, MEM_W).astype(BF16)
    cmv = cache_mem_v.reshape(depth, Bs, M, MEM_W).astype(BF16)
    h1_s, k_s, v_s, kb_s, vb_s, S_s = _layer_a(
        x_sample, cos_s, sin_s, state_ret[0], cmk[0], cmv[0], norm_a[0], wia, woa, norm_kv, wkv,
        NB=nbs, TT=Ts, L=Ts)
    KW = N_HEADS * HEAD_W
    y_s = _layer_b_sample(h1_s, cache_k.reshape(Bs, past, KW), cache_v.reshape(Bs, past, KW), kb_s, vb_s,
                          cmk[1], cmv[1], norm_b[0], wib, diff_lambda[0], subln_b[0], wob, norm_f,
                          TK=_pick(past, (512, 256, 128)), lam_init=lam_init)

    return (y_p, y_s, S_p[None],
            k_p.reshape(Bp, Tp, N_HEADS, 2, DQK), v_p.reshape(Bp, Tp, N_HEADS, HEAD_W),
            mk_p, mv_p, S_s[None],
            k_s.reshape(Bs, Ts, N_HEADS, 2, DQK), v_s.reshape(Bs, Ts, N_HEADS, HEAD_W))
```

```python
import functools
import math

import jax
import jax.numpy as jnp
from jax import lax
from jax.experimental import pallas as pl
from jax.experimental.pallas import tpu as pltpu

F32 = jnp.float32
BF16 = jnp.bfloat16

EPS = 1e-6
ROPE_BASE = 10000.0
CHUNK = 64
N_HEADS = 4
HEAD_W = 256
DQK = 128
MEM_HD = 128
MEM_W = N_HEADS * MEM_HD
LANES = 128
KV_GROUPS = N_HEADS * HEAD_W // LANES
NEG_BIG = -1e30
LOG2E = math.log2(math.e)

VMEM_LIMIT_BYTES = 56 * 1024 * 1024


def _dot(a, b):
    return jnp.dot(a, b, preferred_element_type=F32)


def _dot_nt(a, b):
    return lax.dot_general(a, b, (((1,), (1,)), ((), ())), preferred_element_type=F32)


def _dot_tn(a, b):
    return lax.dot_general(a, b, (((0,), (0,)), ((), ())), preferred_element_type=F32)


def _rms(x):
    return x * lax.rsqrt(jnp.mean(x * x, axis=-1, keepdims=True) + EPS)


def _silu(g):
    return g / (1.0 + jnp.exp(-g))


def _const_spec(shape):
    nd = len(shape)
    return pl.BlockSpec(shape, lambda *_: (0,) * nd, pipeline_mode=pl.Buffered(1))


def _v_row(h, dt):
    return dt * N_HEADS + h


def _mem_attention(qm, gm, mk_ref, mv_ref, nb, rows, cat_scr, col0):
    for hh in range(N_HEADS):
        cs = slice(hh * MEM_HD, (hh + 1) * MEM_HD)
        q = (qm[rows, cs] * (MEM_HD ** -0.5 * LOG2E)).astype(BF16)
        s = _dot_nt(q, mk_ref[nb, :, cs])
        p = jnp.exp2(s - jnp.max(s, axis=-1, keepdims=True))
        l = jnp.sum(p, axis=-1, keepdims=True)
        o = _dot(p.astype(BF16), mv_ref[nb, :, cs]) / l
        cat_scr[rows, col0 + hh * MEM_HD: col0 + (hh + 1) * MEM_HD] = (o * _silu(gm[rows, cs])).astype(BF16)


def _memkv_kernel(mem_ref, g_ref, w_ref, mk_ref, mv_ref, mkb_ref, mvb_ref):
    tm = mem_ref.shape[0]
    hb = (_rms(mem_ref[...]) * g_ref[0]).astype(BF16)
    kv = _dot(hb, w_ref[0])
    mk, mv = kv[:, :MEM_W], kv[:, MEM_W:]
    for hh in range(N_HEADS):
        cs = slice(hh * MEM_HD, (hh + 1) * MEM_HD)
        mk_ref[0, pl.ds(hh, tm, stride=N_HEADS), :] = mk[:, cs]
        mv_ref[0, pl.ds(hh, tm, stride=N_HEADS), :] = mv[:, cs]
    mkb_ref[0] = mk.astype(BF16)
    mvb_ref[0] = mv.astype(BF16)


def _memory_kv(mem, norm_mem, w_mem_kv):
    B, M, D = mem.shape
    nl = norm_mem.shape[0]
    rows = B * M
    tm = 512 if rows % 512 == 0 else M
    f = pl.pallas_call(
        _memkv_kernel,
        grid=(nl, rows // tm),
        in_specs=[
            pl.BlockSpec((tm, D), lambda l, i: (i, 0)),
            pl.BlockSpec((1, 1, D), lambda l, i: (l, 0, 0)),
            pl.BlockSpec((1, D, 2 * MEM_W), lambda l, i: (l, 0, 0)),
        ],
        out_specs=[pl.BlockSpec((1, tm * N_HEADS, MEM_HD), lambda l, i: (l, i, 0))] * 2
        + [pl.BlockSpec((1, tm, MEM_W), lambda l, i: (l, i, 0))] * 2,
        out_shape=[jax.ShapeDtypeStruct((nl, rows * N_HEADS, MEM_HD), F32)] * 2
        + [jax.ShapeDtypeStruct((nl, rows, MEM_W), BF16)] * 2,
        compiler_params=pltpu.CompilerParams(
            dimension_semantics=("arbitrary", "arbitrary"), vmem_limit_bytes=VMEM_LIMIT_BYTES),
        name="memkv",
    )
    mk, mv, mkb, mvb = f(mem.reshape(rows, D), norm_mem.reshape(nl, 1, D), w_mem_kv.astype(BF16))
    shp5 = (nl, B, M, N_HEADS, MEM_HD)
    return mk.reshape(shp5), mv.reshape(shp5), mkb.reshape(nl, B, M, MEM_W), mvb.reshape(nl, B, M, MEM_W)


def _layer_a_kernel(*refs, NB, TT, L, has_state):
    it = iter(refs)
    x_ref, cos_ref, sin_ref = next(it), next(it), next(it)
    s0_ref = next(it) if has_state else None
    mk_ref, mv_ref, na_ref, win_ref, wout_ref, nkv_ref, wkv_ref = (next(it) for _ in range(7))
    h1_ref, k_ref, v_ref, kb_ref, vb_ref, S_ref = (next(it) for _ in range(6))
    h_scr, cat_scr, dec_scr = (next(it) for _ in range(3))

    R = NB * TT
    D = x_ref.shape[-1]
    QW = N_HEADS * HEAD_W
    t = pl.program_id(1)
    log_g = [math.log(1.0 - 2.0 ** (-5.0 - h)) for h in range(N_HEADS)]

    @pl.when((pl.program_id(0) == 0) & (t == 0))
    def _():
        d = (lax.broadcasted_iota(jnp.int32, (L, L), 0) - lax.broadcasted_iota(jnp.int32, (L, L), 1)).astype(F32)
        for h in range(N_HEADS):
            dec_scr[h] = jnp.where(d >= 0, jnp.exp(log_g[h] * jnp.maximum(d, 0.0)), 0.0)

    @pl.when(t == 0)
    def _():
        if has_state:
            S_ref[...] = s0_ref[...]
        else:
            S_ref[...] = jnp.zeros(S_ref.shape, F32)

    x = x_ref[...].reshape(R, D)
    h_scr[...] = (_rms(x) * na_ref[...]).astype(BF16)
    cos = cos_ref[...]
    sin = sin_ref[...]
    half = HEAD_W // 2
    idx = lax.broadcasted_iota(jnp.int32, (L, 1), 0).astype(F32)

    def rope(c0):
        y = _dot(h_scr[...], win_ref[:, c0:c0 + HEAD_W])
        y1, y2 = y[:, :half], y[:, half:]
        return jnp.concatenate([y1 * cos - y2 * sin, y2 * cos + y1 * sin], axis=-1)

    for h in range(N_HEADS):
        q = rope(h * HEAD_W).astype(BF16)
        k = rope(QW + h * HEAD_W) * (HEAD_W ** -0.5)
        v = _dot(h_scr[...], win_ref[:, 2 * QW + h * HEAD_W: 2 * QW + (h + 1) * HEAD_W]).astype(BF16)
        g = _dot(h_scr[...], win_ref[:, 3 * QW + h * HEAD_W: 3 * QW + (h + 1) * HEAD_W])
        a_in = jnp.exp(log_g[h] * (idx + 1.0))
        b_out = jnp.exp(log_g[h] * (L - 1.0 - idx))
        g_blk = math.exp(log_g[h] * L)
        for nb in range(NB):
            for j in range(TT // L):
                rows = slice(nb * TT + j * L, nb * TT + (j + 1) * L)
                qb, vb = q[rows], v[rows]
                kf = k[rows]
                S = S_ref[nb, h]
                sc = _dot_nt(qb, kf.astype(BF16)) * dec_scr[h]
                o = _dot(sc.astype(BF16), vb) + _dot(qb, S.astype(BF16)) * a_in
                S_ref[nb, h] = g_blk * S + _dot_tn((kf * b_out).astype(BF16), vb)
                cat_scr[rows, h * HEAD_W:(h + 1) * HEAD_W] = (_rms(o) * _silu(g[rows])).astype(BF16)

    qm = _dot(h_scr[...], win_ref[:, 4 * QW: 4 * QW + MEM_W])
    gm = _dot(h_scr[...], win_ref[:, 4 * QW + MEM_W: 4 * QW + 2 * MEM_W])
    for nb in range(NB):
        _mem_attention(qm, gm, mk_ref, mv_ref, nb, slice(nb * TT, (nb + 1) * TT), cat_scr, QW)

    NC = 512
    for c in range(D // NC):
        cs = slice(c * NC, (c + 1) * NC)
        y = _dot(cat_scr[...], wout_ref[:, cs]) + x_ref[:, :, cs].reshape(R, NC)
        h1_ref[:, :, cs] = y.reshape(NB, TT, NC)

    hk = (_rms(h1_ref[...].reshape(R, D)) * nkv_ref[...]).astype(BF16)
    KW = kb_ref.shape[-1]
    for c in range(2 * KW // NC):
        kv = _dot(hk, wkv_ref[:, c * NC:(c + 1) * NC])
        is_k = c < KW // NC
        col0 = c * NC if is_k else c * NC - KW
        (kb_ref if is_k else vb_ref)[:, :, col0:col0 + NC] = kv.astype(BF16).reshape(NB, TT, NC)
        for jj in range(NC // LANES):
            col = col0 + jj * LANES
            if is_k:
                row = col // LANES
                k_ref[pl.ds(row, R, stride=KV_GROUPS), :] = kv[:, jj * LANES:(jj + 1) * LANES]
            else:
                row = _v_row(col // HEAD_W, (col % HEAD_W) // LANES)
                v_ref[pl.ds(row, R, stride=KV_GROUPS), :] = kv[:, jj * LANES:(jj + 1) * LANES]


def _layer_a(x, cos, sin, s0, mkb, mvb, norm_a, w_in, w_out, norm_kv, w_kv, *, NB, TT, L):
    B, T, D = x.shape
    R = NB * TT
    nt = T // TT
    assert NB == 1 or nt == 1
    KW = w_kv.shape[1] // 2
    M = mkb.shape[1]
    has_state = s0 is not None
    bt = lambda b, t: (b, t, 0)
    b0 = lambda b, t: (b, 0, 0)
    in_specs = [pl.BlockSpec((NB, TT, D), bt),
                pl.BlockSpec((R, DQK), lambda b, t: (t, 0)),
                pl.BlockSpec((R, DQK), lambda b, t: (t, 0))]
    args = [x, cos, sin]
    if has_state:
        in_specs.append(pl.BlockSpec((NB, N_HEADS, HEAD_W, HEAD_W), lambda b, t: (b, 0, 0, 0)))
        args.append(s0)
    in_specs += [pl.BlockSpec((NB, M, MEM_W), b0), pl.BlockSpec((NB, M, MEM_W), b0),
                 _const_spec((1, D)), _const_spec(w_in.shape), _const_spec(w_out.shape),
                 _const_spec((1, D)), _const_spec(w_kv.shape)]
    args += [mkb, mvb, norm_a.reshape(1, D), w_in, w_out, norm_kv.reshape(1, D), w_kv]
    rows_spec = pl.BlockSpec((R * KV_GROUPS, LANES), lambda b, t: (b * nt + t, 0))
    out_specs = [pl.BlockSpec((NB, TT, D), bt), rows_spec, rows_spec,
                 pl.BlockSpec((NB, TT, KW), bt), pl.BlockSpec((NB, TT, KW), bt),
                 pl.BlockSpec((NB, N_HEADS, HEAD_W, HEAD_W), lambda b, t: (b, 0, 0, 0))]
    out_shape = [jax.ShapeDtypeStruct((B, T, D), F32)] + [
        jax.ShapeDtypeStruct((B * T * KV_GROUPS, LANES), F32)] * 2 + [
        jax.ShapeDtypeStruct((B, T, KW), BF16)] * 2 + [jax.ShapeDtypeStruct((B, N_HEADS, HEAD_W, HEAD_W), F32)]
    f = pl.pallas_call(
        functools.partial(_layer_a_kernel, NB=NB, TT=TT, L=L, has_state=has_state),
        grid=(B // NB, nt),
        in_specs=in_specs,
        out_specs=out_specs,
        out_shape=out_shape,
        scratch_shapes=[pltpu.VMEM((R, D), BF16), pltpu.VMEM((R, N_HEADS * HEAD_W + MEM_W), BF16),
                        pltpu.VMEM((N_HEADS, L, L), F32)],
        compiler_params=pltpu.CompilerParams(
            dimension_semantics=("arbitrary", "arbitrary"), vmem_limit_bytes=VMEM_LIMIT_BYTES),
        name="layer_a",
    )
    return f(*args)


def _k_view_to_5d(kr, B, T):
    return kr.reshape(B, T, N_HEADS, 2, DQK)


def _v_view_to_4d(vr, B, T):
    return vr.reshape(B, T, 2, N_HEADS, LANES).transpose(0, 1, 3, 2, 4).reshape(B, T, N_HEADS, HEAD_W)


def _diff_lambda(lp_ref, lam_init):
    lp = lp_ref[...]
    a = jnp.sum(lp[0:1] * lp[1:2], axis=-1, keepdims=True)
    b = jnp.sum(lp[2:3] * lp[3:4], axis=-1, keepdims=True)
    return jnp.exp(a) - jnp.exp(b) + lam_init


def _diff_project_q(h1, nb_ref, win_ref, h_scr, q_scr):
    h_scr[...] = (_rms(h1) * nb_ref[...]).astype(BF16)
    for hh in range(N_HEADS):
        q = _dot(h_scr[...], win_ref[:, hh * HEAD_W:(hh + 1) * HEAD_W]) * (DQK ** -0.5 * LOG2E)
        q_scr[hh] = q.astype(BF16)


def _diff_finish(h1, o_of, lam_init, h_scr, win_ref, sub_ref, mk_ref, mv_ref, wout_ref, nf_ref, cat_scr, y_ref):
    TQ, D = h1.shape
    QW = N_HEADS * HEAD_W
    for hh in range(N_HEADS):
        o = _rms(o_of(hh)) * sub_ref[...] * (1.0 - lam_init)
        g = _dot(h_scr[...], win_ref[:, QW + hh * HEAD_W: QW + (hh + 1) * HEAD_W])
        cat_scr[:, hh * HEAD_W:(hh + 1) * HEAD_W] = (o * _silu(g)).astype(BF16)
    qm = _dot(h_scr[...], win_ref[:, 2 * QW: 2 * QW + MEM_W])
    gm = _dot(h_scr[...], win_ref[:, 2 * QW + MEM_W: 2 * QW + 2 * MEM_W])
    _mem_attention(qm, gm, mk_ref, mv_ref, 0, slice(0, TQ), cat_scr, QW)
    NC = 512
    for c in range(D // NC):
        cs = slice(c * NC, (c + 1) * NC)
        y_ref[0, :, cs] = _dot(cat_scr[...], wout_ref[:, cs]) + h1[:, cs]
    y_ref[0] = _rms(y_ref[0]) * nf_ref[...]


def _layer_b_prompt_kernel(h1_ref, kb_ref, vb_ref, mk_ref, mv_ref, nb_ref, win_ref, lp_ref, sub_ref, wout_ref,
                           nf_ref, y_ref, h_scr, q_scr, o_scr, cat_scr, *, TQ, NQ, lam_init):
    qi = pl.program_id(1)
    h1 = h1_ref[0]
    _diff_project_q(h1, nb_ref, win_ref, h_scr, q_scr)
    lam = _diff_lambda(lp_ref, lam_init)
    diag = (lax.broadcasted_iota(jnp.int32, (TQ, TQ), 1) // CHUNK) <= (
        lax.broadcasted_iota(jnp.int32, (TQ, TQ), 0) // CHUNK)

    for case in range(NQ):
        @pl.when(qi == case)
        def _(case=case):
            kt = (case + 1) * TQ
            for hh in range(N_HEADS):
                hs = slice(hh * HEAD_W, (hh + 1) * HEAD_W)
                ps, ls = [], []
                for c in range(2):
                    s = _dot_nt(q_scr[hh, :, c * DQK:(c + 1) * DQK],
                                kb_ref[0, :kt, hh * HEAD_W + c * DQK: hh * HEAD_W + (c + 1) * DQK])
                    last = jnp.where(diag, s[:, kt - TQ:], NEG_BIG)
                    s = last if case == 0 else jnp.concatenate([s[:, :kt - TQ], last], axis=1)
                    p = jnp.exp2(s - jnp.max(s, axis=-1, keepdims=True))
                    ls.append(jnp.sum(p, axis=-1, keepdims=True))
                    ps.append(p.astype(BF16))
                pv = _dot(jnp.concatenate(ps, axis=0), vb_ref[0, :kt, hs])
                o_scr[hh] = pv[:TQ] / ls[0] - lam * (pv[TQ:] / ls[1])

    _diff_finish(h1, lambda hh: o_scr[hh], lam_init, h_scr, win_ref, sub_ref, mk_ref, mv_ref, wout_ref, nf_ref,
                 cat_scr, y_ref)


def _layer_b_prompt(h1, kb, vb, mkb, mvb, norm_b, w_in, lam_p, subln, w_out, norm_f, *, TQ, lam_init):
    B, T, D = h1.shape
    KW = kb.shape[-1]
    M = mkb.shape[1]
    b0 = lambda b, q: (b, 0, 0)
    f = pl.pallas_call(
        functools.partial(_layer_b_prompt_kernel, TQ=TQ, NQ=T // TQ, lam_init=lam_init),
        grid=(B, T // TQ),
        in_specs=[pl.BlockSpec((1, TQ, D), lambda b, q: (b, q, 0)),
                  pl.BlockSpec((1, T, KW), b0), pl.BlockSpec((1, T, KW), b0),
                  pl.BlockSpec((1, M, MEM_W), b0), pl.BlockSpec((1, M, MEM_W), b0),
                  _const_spec((1, D)), _const_spec(w_in.shape), _const_spec(lam_p.shape),
                  _const_spec((1, HEAD_W)), _const_spec(w_out.shape), _const_spec((1, D))],
        out_specs=pl.BlockSpec((1, TQ, D), lambda b, q: (b, q, 0)),
        out_shape=jax.ShapeDtypeStruct((B, T, D), F32),
        scratch_shapes=[pltpu.VMEM((TQ, D), BF16), pltpu.VMEM((N_HEADS, TQ, HEAD_W), BF16),
                        pltpu.VMEM((N_HEADS, TQ, HEAD_W), F32),
                        pltpu.VMEM((TQ, N_HEADS * HEAD_W + MEM_W), BF16)],
        compiler_params=pltpu.CompilerParams(
            dimension_semantics=("arbitrary", "arbitrary"), vmem_limit_bytes=VMEM_LIMIT_BYTES),
        name="layer_b_prompt",
    )
    return f(h1, kb, vb, mkb, mvb, norm_b.reshape(1, D), w_in, lam_p, subln.reshape(1, HEAD_W), w_out,
             norm_f.reshape(1, D))


def _online_tile(q_scr, k_of, v_of, m_scr, l_scr, acc_scr):
    TQ = q_scr.shape[1]
    for hh in range(N_HEADS):
        ps, alphas = [], []
        for c in range(2):
            i = 2 * hh + c
            s = _dot_nt(q_scr[hh, :, c * DQK:(c + 1) * DQK], k_of(hh, c))
            m_prev = m_scr[i]
            m_new = jnp.maximum(m_prev, jnp.max(s, axis=-1, keepdims=True))
            alpha = jnp.exp2(m_prev - m_new)
            p = jnp.exp2(s - m_new)
            l_scr[i] = alpha * l_scr[i] + jnp.sum(p, axis=-1, keepdims=True)
            m_scr[i] = m_new
            ps.append(p.astype(BF16))
            alphas.append(alpha)
        pv = _dot(jnp.concatenate(ps, axis=0), v_of(hh))
        acc_scr[2 * hh] = alphas[0] * acc_scr[2 * hh] + pv[:TQ]
        acc_scr[2 * hh + 1] = alphas[1] * acc_scr[2 * hh + 1] + pv[TQ:]


def _layer_b_sample_kernel(h1_ref, ck_ref, cv_ref, kn_ref, vn_ref, mk_ref, mv_ref, nb_ref, win_ref, lp_ref, sub_ref,
                           wout_ref, nf_ref, y_ref, h_scr, q_scr, m_scr, l_scr, acc_scr, cat_scr, *, TK, lam_init):
    kt = pl.program_id(1)

    @pl.when(kt == 0)
    def _():
        _diff_project_q(h1_ref[0], nb_ref, win_ref, h_scr, q_scr)
        m_scr[...] = jnp.full(m_scr.shape, NEG_BIG, F32)
        l_scr[...] = jnp.zeros(l_scr.shape, F32)
        acc_scr[...] = jnp.zeros(acc_scr.shape, F32)

    def ck_of(hh, c):
        return ck_ref[pl.ds(2 * hh + c, TK, stride=KV_GROUPS), :].astype(BF16)

    def cv_of(hh):
        return jnp.concatenate([cv_ref[pl.ds(_v_row(hh, dt), TK, stride=KV_GROUPS), :] for dt in range(2)],
                               axis=1).astype(BF16)

    _online_tile(q_scr, ck_of, cv_of, m_scr, l_scr, acc_scr)

    @pl.when(kt == pl.num_programs(1) - 1)
    def _():
        _online_tile(q_scr,
                     lambda hh, c: kn_ref[0, :, hh * HEAD_W + c * DQK: hh * HEAD_W + (c + 1) * DQK],
                     lambda hh: vn_ref[0, :, hh * HEAD_W:(hh + 1) * HEAD_W],
                     m_scr, l_scr, acc_scr)
        lam = _diff_lambda(lp_ref, lam_init)
        o_of = lambda hh: (acc_scr[2 * hh] / l_scr[2 * hh]
                           - lam * (acc_scr[2 * hh + 1] / l_scr[2 * hh + 1]))
        _diff_finish(h1_ref[0], o_of, lam_init, h_scr, win_ref, sub_ref, mk_ref, mv_ref, wout_ref, nf_ref,
                     cat_scr, y_ref)


def _layer_b_sample(h1, ck_rows, cv_rows, kb, vb, mkb, mvb, norm_b, w_in, lam_p, subln, w_out, norm_f, *, P, TK,
                    lam_init):
    B, T, D = h1.shape
    KW = kb.shape[-1]
    M = mkb.shape[1]
    nkt = P // TK
    b0 = lambda b, k: (b, 0, 0)
    cache_spec = pl.BlockSpec((TK * KV_GROUPS, LANES), lambda b, k: (b * nkt + k, 0))
    f = pl.pallas_call(
        functools.partial(_layer_b_sample_kernel, TK=TK, lam_init=lam_init),
        grid=(B, nkt),
        in_specs=[pl.BlockSpec((1, T, D), b0), cache_spec, cache_spec,
                  pl.BlockSpec((1, T, KW), b0), pl.BlockSpec((1, T, KW), b0),
                  pl.BlockSpec((1, M, MEM_W), b0), pl.BlockSpec((1, M, MEM_W), b0),
                  _const_spec((1, D)), _const_spec(w_in.shape), _const_spec(lam_p.shape),
                  _const_spec((1, HEAD_W)), _const_spec(w_out.shape), _const_spec((1, D))],
        out_specs=pl.BlockSpec((1, T, D), b0),
        out_shape=jax.ShapeDtypeStruct((B, T, D), F32),
        scratch_shapes=[pltpu.VMEM((T, D), BF16), pltpu.VMEM((N_HEADS, T, HEAD_W), BF16),
                        pltpu.VMEM((2 * N_HEADS, T, 1), F32), pltpu.VMEM((2 * N_HEADS, T, 1), F32),
                        pltpu.VMEM((2 * N_HEADS, T, HEAD_W), F32),
                        pltpu.VMEM((T, N_HEADS * HEAD_W + MEM_W), BF16)],
        compiler_params=pltpu.CompilerParams(
            dimension_semantics=("arbitrary", "arbitrary"), vmem_limit_bytes=VMEM_LIMIT_BYTES),
        name="layer_b_sample",
    )
    return f(h1, ck_rows, cv_rows, kb, vb, mkb, mvb, norm_b.reshape(1, D), w_in, lam_p, subln.reshape(1, HEAD_W),
             w_out, norm_f.reshape(1, D))


def _rope_tables(pos):
    half = DQK
    inv = ROPE_BASE ** (-jnp.arange(half, dtype=F32) / half)
    ang = pos.astype(F32)[:, None] * inv[None, :]
    return jnp.cos(ang), jnp.sin(ang)


def _pick(n, candidates):
    for c in candidates:
        if n % c == 0:
            return c
    return n


def kernel(x_prompt, x_sample, state_ret, cache_k, cache_v, cache_mem_k, cache_mem_v, mem_prompt,
           norm_a, w_in_a, w_out_a, norm_kv, w_kv, norm_b, w_in_b, diff_lambda, subln_b, w_out_b,
           norm_mem, w_mem_kv, norm_f):
    depth = norm_mem.shape[0]
    assert norm_a.shape[0] == 1 and norm_b.shape[0] == 1 and depth == 2, "kernel is written for depth 2"
    lam_init = 0.8 - 0.6 * math.exp(-0.3 * 1)
    wia, woa, wkv = w_in_a[0].astype(BF16), w_out_a[0].astype(BF16), w_kv.astype(BF16)
    wib, wob = w_in_b[0].astype(BF16), w_out_b[0].astype(BF16)

    Bp, Tp, D = x_prompt.shape
    mk_p, mv_p, mkb_p, mvb_p = _memory_kv(mem_prompt, norm_mem, w_mem_kv)
    cos_p, sin_p = _rope_tables(jnp.arange(Tp))
    tt = _pick(Tp, (256,))
    h1_p, k_p, v_p, kb_p, vb_p, S_p = _layer_a(
        x_prompt, cos_p, sin_p, None, mkb_p[0], mvb_p[0], norm_a[0], wia, woa, norm_kv, wkv, NB=1, TT=tt, L=tt)
    y_p = _layer_b_prompt(h1_p, kb_p, vb_p, mkb_p[1], mvb_p[1], norm_b[0], wib, diff_lambda[0], subln_b[0], wob,
                          norm_f, TQ=tt, lam_init=lam_init)

    Bs, Ts, _ = x_sample.shape
    past = cache_k.shape[1]
    nbs = _pick(Bs, (4, 2, 1))
    cos_s, sin_s = _rope_tables(past + jnp.arange(Ts))
    cos_s, sin_s = jnp.tile(cos_s, (nbs, 1)), jnp.tile(sin_s, (nbs, 1))
    M = cache_mem_k.shape[2]
    cmk = cache_mem_k.reshape(depth, Bs, M, MEM_W).astype(BF16)
    cmv = cache_mem_v.reshape(depth, Bs, M, MEM_W).astype(BF16)
    h1_s, k_s, v_s, kb_s, vb_s, S_s = _layer_a(
        x_sample, cos_s, sin_s, state_ret[0], cmk[0], cmv[0], norm_a[0], wia, woa, norm_kv, wkv,
        NB=nbs, TT=Ts, L=Ts)
    ck_rows = cache_k.reshape(Bs * past * KV_GROUPS, LANES)
    cv_rows = cache_v.reshape(Bs, past, N_HEADS, 2, LANES).transpose(0, 1, 3, 2, 4).reshape(
        Bs * past * KV_GROUPS, LANES)
    y_s = _layer_b_sample(h1_s, ck_rows, cv_rows, kb_s, vb_s, cmk[1], cmv[1], norm_b[0], wib, diff_lambda[0],
                          subln_b[0], wob, norm_f, P=past, TK=_pick(past, (512, 256, 128)), lam_init=lam_init)

    return (y_p, y_s, S_p[None], _k_view_to_5d(k_p, Bp, Tp), _v_view_to_4d(v_p, Bp, Tp),
            mk_p, mv_p, S_s[None], _k_view_to_5d(k_s, Bs, Ts), _v_view_to_4d(v_s, Bs, Ts))
```

```python
import functools
import math

import jax
import jax.numpy as jnp
from jax import lax
from jax.experimental import pallas as pl
from jax.experimental.pallas import tpu as pltpu

F32 = jnp.float32
BF16 = jnp.bfloat16

EPS = 1e-6
ROPE_BASE = 10000.0
CHUNK = 64
N_HEADS = 4
HEAD_W = 256
DQK = 128
MEM_HD = 128
MEM_W = N_HEADS * MEM_HD
LANES = 128
KV_GROUPS = N_HEADS * HEAD_W // LANES
NEG_BIG = -1e30
LOG2E = math.log2(math.e)

VMEM_LIMIT_BYTES = 56 * 1024 * 1024


def _dot(a, b):
    return jnp.dot(a, b, preferred_element_type=F32)


def _dot_nt(a, b):
    return lax.dot_general(a, b, (((1,), (1,)), ((), ())), preferred_element_type=F32)


def _dot_tn(a, b):
    return lax.dot_general(a, b, (((0,), (0,)), ((), ())), preferred_element_type=F32)


def _rms(x):
    return x * lax.rsqrt(jnp.mean(x * x, axis=-1, keepdims=True) + EPS)


def _silu(g):
    return g / (1.0 + jnp.exp(-g))


def _const_spec(shape):
    nd = len(shape)
    return pl.BlockSpec(shape, lambda *_: (0,) * nd, pipeline_mode=pl.Buffered(1))


def _v_row(h, dt):
    return dt * N_HEADS + h


def _mem_attention(qm, gm, mk_ref, mv_ref, nb, rows, cat_scr, col0):
    for hh in range(N_HEADS):
        cs = slice(hh * MEM_HD, (hh + 1) * MEM_HD)
        q = (qm[rows, cs] * (MEM_HD ** -0.5 * LOG2E)).astype(BF16)
        s = _dot_nt(q, mk_ref[nb, :, cs])
        p = jnp.exp2(s - jnp.max(s, axis=-1, keepdims=True))
        l = jnp.sum(p, axis=-1, keepdims=True)
        o = _dot(p.astype(BF16), mv_ref[nb, :, cs]) / l
        cat_scr[rows, col0 + hh * MEM_HD: col0 + (hh + 1) * MEM_HD] = (o * _silu(gm[rows, cs])).astype(BF16)


def _memkv_kernel(mem_ref, g_ref, w_ref, mk_ref, mv_ref, mkb_ref, mvb_ref):
    tm = mem_ref.shape[0]
    hb = (_rms(mem_ref[...]) * g_ref[0]).astype(BF16)
    kv = _dot(hb, w_ref[0])
    mk, mv = kv[:, :MEM_W], kv[:, MEM_W:]
    for hh in range(N_HEADS):
        cs = slice(hh * MEM_HD, (hh + 1) * MEM_HD)
        mk_ref[0, pl.ds(hh, tm, stride=N_HEADS), :] = mk[:, cs]
        mv_ref[0, pl.ds(hh, tm, stride=N_HEADS), :] = mv[:, cs]
    mkb_ref[0] = mk.astype(BF16)
    mvb_ref[0] = mv.astype(BF16)


def _memory_kv(mem, norm_mem, w_mem_kv):
    B, M, D = mem.shape
    nl = norm_mem.shape[0]
    rows = B * M
    tm = 512 if rows % 512 == 0 else M
    f = pl.pallas_call(
        _memkv_kernel,
        grid=(nl, rows // tm),
        in_specs=[
            pl.BlockSpec((tm, D), lambda l, i: (i, 0)),
            pl.BlockSpec((1, 1, D), lambda l, i: (l, 0, 0)),
            pl.BlockSpec((1, D, 2 * MEM_W), lambda l, i: (l, 0, 0)),
        ],
        out_specs=[pl.BlockSpec((1, tm * N_HEADS, MEM_HD), lambda l, i: (l, i, 0))] * 2
        + [pl.BlockSpec((1, tm, MEM_W), lambda l, i: (l, i, 0))] * 2,
        out_shape=[jax.ShapeDtypeStruct((nl, rows * N_HEADS, MEM_HD), F32)] * 2
        + [jax.ShapeDtypeStruct((nl, rows, MEM_W), BF16)] * 2,
        compiler_params=pltpu.CompilerParams(
            dimension_semantics=("arbitrary", "arbitrary"), vmem_limit_bytes=VMEM_LIMIT_BYTES),
        name="memkv",
    )
    mk, mv, mkb, mvb = f(mem.reshape(rows, D), norm_mem.reshape(nl, 1, D), w_mem_kv.astype(BF16))
    shp5 = (nl, B, M, N_HEADS, MEM_HD)
    return mk.reshape(shp5), mv.reshape(shp5), mkb.reshape(nl, B, M, MEM_W), mvb.reshape(nl, B, M, MEM_W)


def _layer_a_kernel(*refs, NB, TT, L, has_state):
    it = iter(refs)
    x_ref, cos_ref, sin_ref = next(it), next(it), next(it)
    s0_ref = next(it) if has_state else None
    mk_ref, mv_ref, na_ref, win_ref, wout_ref, nkv_ref, wkv_ref = (next(it) for _ in range(7))
    h1_ref, k_ref, v_ref, kb_ref, vb_ref, S_ref = (next(it) for _ in range(6))
    h_scr, cat_scr, dec_scr = (next(it) for _ in range(3))

    R = NB * TT
    D = x_ref.shape[-1]
    QW = N_HEADS * HEAD_W
    t = pl.program_id(1)
    log_g = [math.log(1.0 - 2.0 ** (-5.0 - h)) for h in range(N_HEADS)]

    @pl.when((pl.program_id(0) == 0) & (t == 0))
    def _():
        d = (lax.broadcasted_iota(jnp.int32, (L, L), 0) - lax.broadcasted_iota(jnp.int32, (L, L), 1)).astype(F32)
        for h in range(N_HEADS):
            dec_scr[h] = jnp.where(d >= 0, jnp.exp(log_g[h] * jnp.maximum(d, 0.0)), 0.0)

    @pl.when(t == 0)
    def _():
        if has_state:
            S_ref[...] = s0_ref[...]
        else:
            S_ref[...] = jnp.zeros(S_ref.shape, F32)

    x = x_ref[...].reshape(R, D)
    h_scr[...] = (_rms(x) * na_ref[...]).astype(BF16)
    cos = cos_ref[...]
    sin = sin_ref[...]
    half = HEAD_W // 2
    idx = lax.broadcasted_iota(jnp.int32, (L, 1), 0).astype(F32)

    def rope(c0):
        y = _dot(h_scr[...], win_ref[:, c0:c0 + HEAD_W])
        y1, y2 = y[:, :half], y[:, half:]
        return jnp.concatenate([y1 * cos - y2 * sin, y2 * cos + y1 * sin], axis=-1)

    for h in range(N_HEADS):
        q = rope(h * HEAD_W).astype(BF16)
        k = rope(QW + h * HEAD_W) * (HEAD_W ** -0.5)
        v = _dot(h_scr[...], win_ref[:, 2 * QW + h * HEAD_W: 2 * QW + (h + 1) * HEAD_W]).astype(BF16)
        g = _dot(h_scr[...], win_ref[:, 3 * QW + h * HEAD_W: 3 * QW + (h + 1) * HEAD_W])
        a_in = jnp.exp(log_g[h] * (idx + 1.0))
        b_out = jnp.exp(log_g[h] * (L - 1.0 - idx))
        g_blk = math.exp(log_g[h] * L)
        for nb in range(NB):
            for j in range(TT // L):
                rows = slice(nb * TT + j * L, nb * TT + (j + 1) * L)
                qb, vb = q[rows], v[rows]
                kf = k[rows]
                S = S_ref[nb, h]
                sc = _dot_nt(qb, kf.astype(BF16)) * dec_scr[h]
                o = _dot(sc.astype(BF16), vb) + _dot(qb, S.astype(BF16)) * a_in
                S_ref[nb, h] = g_blk * S + _dot_tn((kf * b_out).astype(BF16), vb)
                cat_scr[rows, h * HEAD_W:(h + 1) * HEAD_W] = (_rms(o) * _silu(g[rows])).astype(BF16)

    qm = _dot(h_scr[...], win_ref[:, 4 * QW: 4 * QW + MEM_W])
    gm = _dot(h_scr[...], win_ref[:, 4 * QW + MEM_W: 4 * QW + 2 * MEM_W])
    for nb in range(NB):
        _mem_attention(qm, gm, mk_ref, mv_ref, nb, slice(nb * TT, (nb + 1) * TT), cat_scr, QW)

    NC = 512
    for c in range(D // NC):
        cs = slice(c * NC, (c + 1) * NC)
        y = _dot(cat_scr[...], wout_ref[:, cs]) + x_ref[:, :, cs].reshape(R, NC)
        h1_ref[:, :, cs] = y.reshape(NB, TT, NC)

    hk = (_rms(h1_ref[...].reshape(R, D)) * nkv_ref[...]).astype(BF16)
    KW = kb_ref.shape[-1]
    for c in range(2 * KW // NC):
        kv = _dot(hk, wkv_ref[:, c * NC:(c + 1) * NC])
        is_k = c < KW // NC
        col0 = c * NC if is_k else c * NC - KW
        (kb_ref if is_k else vb_ref)[:, :, col0:col0 + NC] = kv.astype(BF16).reshape(NB, TT, NC)
        for jj in range(NC // LANES):
            col = col0 + jj * LANES
            if is_k:
                row = col // LANES
                k_ref[pl.ds(row, R, stride=KV_GROUPS), :] = kv[:, jj * LANES:(jj + 1) * LANES]
            else:
                row = _v_row(col // HEAD_W, (col % HEAD_W) // LANES)
                v_ref[pl.ds(row, R, stride=KV_GROUPS), :] = kv[:, jj * LANES:(jj + 1) * LANES]


def _layer_a(x, cos, sin, s0, mkb, mvb, norm_a, w_in, w_out, norm_kv, w_kv, *, NB, TT, L):
    B, T, D = x.shape
    R = NB * TT
    nt = T // TT
    assert NB == 1 or nt == 1
    KW = w_kv.shape[1] // 2
    M = mkb.shape[1]
    has_state = s0 is not None
    bt = lambda b, t: (b, t, 0)
    b0 = lambda b, t: (b, 0, 0)
    in_specs = [pl.BlockSpec((NB, TT, D), bt),
                pl.BlockSpec((R, DQK), lambda b, t: (t, 0)),
                pl.BlockSpec((R, DQK), lambda b, t: (t, 0))]
    args = [x, cos, sin]
    if has_state:
        in_specs.append(pl.BlockSpec((NB, N_HEADS, HEAD_W, HEAD_W), lambda b, t: (b, 0, 0, 0)))
        args.append(s0)
    in_specs += [pl.BlockSpec((NB, M, MEM_W), b0), pl.BlockSpec((NB, M, MEM_W), b0),
                 _const_spec((1, D)), _const_spec(w_in.shape), _const_spec(w_out.shape),
                 _const_spec((1, D)), _const_spec(w_kv.shape)]
    args += [mkb, mvb, norm_a.reshape(1, D), w_in, w_out, norm_kv.reshape(1, D), w_kv]
    rows_spec = pl.BlockSpec((R * KV_GROUPS, LANES), lambda b, t: (b * nt + t, 0))
    out_specs = [pl.BlockSpec((NB, TT, D), bt), rows_spec, rows_spec,
                 pl.BlockSpec((NB, TT, KW), bt), pl.BlockSpec((NB, TT, KW), bt),
                 pl.BlockSpec((NB, N_HEADS, HEAD_W, HEAD_W), lambda b, t: (b, 0, 0, 0))]
    out_shape = [jax.ShapeDtypeStruct((B, T, D), F32)] + [
        jax.ShapeDtypeStruct((B * T * KV_GROUPS, LANES), F32)] * 2 + [
        jax.ShapeDtypeStruct((B, T, KW), BF16)] * 2 + [jax.ShapeDtypeStruct((B, N_HEADS, HEAD_W, HEAD_W), F32)]
    f = pl.pallas_call(
        functools.partial(_layer_a_kernel, NB=NB, TT=TT, L=L, has_state=has_state),
        grid=(B // NB, nt),
        in_specs=in_specs,
        out_specs=out_specs,
        out_shape=out_shape,
        scratch_shapes=[pltpu.VMEM((R, D), BF16), pltpu.VMEM((R, N_HEADS * HEAD_W + MEM_W), BF16),
                        pltpu.VMEM((N_HEADS, L, L), F32)],
        compiler_params=pltpu.CompilerParams(
            dimension_semantics=("arbitrary", "arbitrary"), vmem_limit_bytes=VMEM_LIMIT_BYTES),
        name="layer_a",
    )
    return f(*args)


def _k_view_to_5d(kr, B, T):
    return kr.reshape(B, T, N_HEADS, 2, DQK)


def _v_view_to_4d(vr, B, T):
    return vr.reshape(B, T, 2, N_HEADS, LANES).transpose(0, 1, 3, 2, 4).reshape(B, T, N_HEADS, HEAD_W)


def _diff_lambda(lp_ref, lam_init):
    lp = lp_ref[...]
    a = jnp.sum(lp[0:1] * lp[1:2], axis=-1, keepdims=True)
    b = jnp.sum(lp[2:3] * lp[3:4], axis=-1, keepdims=True)
    return jnp.exp(a) - jnp.exp(b) + lam_init


def _diff_project_q(h1, nb_ref, win_ref, h_scr, q_scr):
    h_scr[...] = (_rms(h1) * nb_ref[...]).astype(BF16)
    for hh in range(N_HEADS):
        q = _dot(h_scr[...], win_ref[:, hh * HEAD_W:(hh + 1) * HEAD_W]) * (DQK ** -0.5 * LOG2E)
        q_scr[hh] = q.astype(BF16)


def _diff_finish(h1, o_of, lam_init, h_scr, win_ref, sub_ref, mk_ref, mv_ref, wout_ref, nf_ref, cat_scr, y_ref):
    TQ, D = h1.shape
    QW = N_HEADS * HEAD_W
    for hh in range(N_HEADS):
        o = _rms(o_of(hh)) * sub_ref[...] * (1.0 - lam_init)
        g = _dot(h_scr[...], win_ref[:, QW + hh * HEAD_W: QW + (hh + 1) * HEAD_W])
        cat_scr[:, hh * HEAD_W:(hh + 1) * HEAD_W] = (o * _silu(g)).astype(BF16)
    qm = _dot(h_scr[...], win_ref[:, 2 * QW: 2 * QW + MEM_W])
    gm = _dot(h_scr[...], win_ref[:, 2 * QW + MEM_W: 2 * QW + 2 * MEM_W])
    _mem_attention(qm, gm, mk_ref, mv_ref, 0, slice(0, TQ), cat_scr, QW)
    NC = 512
    for c in range(D // NC):
        cs = slice(c * NC, (c + 1) * NC)
        y_ref[0, :, cs] = _dot(cat_scr[...], wout_ref[:, cs]) + h1[:, cs]
    y_ref[0] = _rms(y_ref[0]) * nf_ref[...]


def _layer_b_prompt_kernel(h1_ref, kb_ref, vb_ref, mk_ref, mv_ref, nb_ref, win_ref, lp_ref, sub_ref, wout_ref,
                           nf_ref, y_ref, h_scr, q_scr, o_scr, cat_scr, *, TQ, SUB, NQ, lam_init):
    qi = pl.program_id(1)
    h1 = h1_ref[0]
    _diff_project_q(h1, nb_ref, win_ref, h_scr, q_scr)
    lam = _diff_lambda(lp_ref, lam_init)
    diag = (lax.broadcasted_iota(jnp.int32, (SUB, SUB), 1) // CHUNK) <= (
        lax.broadcasted_iota(jnp.int32, (SUB, SUB), 0) // CHUNK)

    for case in range(NQ):
        @pl.when(qi == case)
        def _(case=case):
            for sub in range(TQ // SUB):
                rows = slice(sub * SUB, (sub + 1) * SUB)
                kt = case * TQ + (sub + 1) * SUB
                for hh in range(N_HEADS):
                    hs = slice(hh * HEAD_W, (hh + 1) * HEAD_W)
                    ps = []
                    for c in range(2):
                        s = _dot_nt(q_scr[hh, rows, c * DQK:(c + 1) * DQK],
                                    kb_ref[0, :kt, hh * HEAD_W + c * DQK: hh * HEAD_W + (c + 1) * DQK])
                        last = jnp.where(diag, s[:, kt - SUB:], NEG_BIG)
                        s = last if kt == SUB else jnp.concatenate([s[:, :kt - SUB], last], axis=1)
                        p = jnp.exp2(s - jnp.max(s, axis=-1, keepdims=True))
                        ps.append(p * (1.0 / jnp.sum(p, axis=-1, keepdims=True)))
                    a = (ps[0] - lam * ps[1]).astype(BF16)
                    o_scr[hh, rows] = _dot(a, vb_ref[0, :kt, hs])

    _diff_finish(h1, lambda hh: o_scr[hh], lam_init, h_scr, win_ref, sub_ref, mk_ref, mv_ref, wout_ref, nf_ref,
                 cat_scr, y_ref)


def _layer_b_prompt(h1, kb, vb, mkb, mvb, norm_b, w_in, lam_p, subln, w_out, norm_f, *, TQ, SUB, lam_init):
    B, T, D = h1.shape
    KW = kb.shape[-1]
    M = mkb.shape[1]
    b0 = lambda b, q: (b, 0, 0)
    f = pl.pallas_call(
        functools.partial(_layer_b_prompt_kernel, TQ=TQ, SUB=SUB, NQ=T // TQ, lam_init=lam_init),
        grid=(B, T // TQ),
        in_specs=[pl.BlockSpec((1, TQ, D), lambda b, q: (b, q, 0)),
                  pl.BlockSpec((1, T, KW), b0), pl.BlockSpec((1, T, KW), b0),
                  pl.BlockSpec((1, M, MEM_W), b0), pl.BlockSpec((1, M, MEM_W), b0),
                  _const_spec((1, D)), _const_spec(w_in.shape), _const_spec(lam_p.shape),
                  _const_spec((1, HEAD_W)), _const_spec(w_out.shape), _const_spec((1, D))],
        out_specs=pl.BlockSpec((1, TQ, D), lambda b, q: (b, q, 0)),
        out_shape=jax.ShapeDtypeStruct((B, T, D), F32),
        scratch_shapes=[pltpu.VMEM((TQ, D), BF16), pltpu.VMEM((N_HEADS, TQ, HEAD_W), BF16),
                        pltpu.VMEM((N_HEADS, TQ, HEAD_W), F32),
                        pltpu.VMEM((TQ, N_HEADS * HEAD_W + MEM_W), BF16)],
        compiler_params=pltpu.CompilerParams(
            dimension_semantics=("arbitrary", "arbitrary"), vmem_limit_bytes=VMEM_LIMIT_BYTES),
        name="layer_b_prompt",
    )
    return f(h1, kb, vb, mkb, mvb, norm_b.reshape(1, D), w_in, lam_p, subln.reshape(1, HEAD_W), w_out,
             norm_f.reshape(1, D))


def _online_tile(q_scr, k_of, v_of, m_scr, l_scr, acc_scr):
    TQ = q_scr.shape[1]
    for hh in range(N_HEADS):
        ps, alphas = [], []
        for c in range(2):
            i = 2 * hh + c
            s = _dot_nt(q_scr[hh, :, c * DQK:(c + 1) * DQK], k_of(hh, c))
            m_prev = m_scr[i]
            m_new = jnp.maximum(m_prev, jnp.max(s, axis=-1, keepdims=True))
            alpha = jnp.exp2(m_prev - m_new)
            p = jnp.exp2(s - m_new)
            l_scr[i] = alpha * l_scr[i] + jnp.sum(p, axis=-1, keepdims=True)
            m_scr[i] = m_new
            ps.append(p.astype(BF16))
            alphas.append(alpha)
        pv = _dot(jnp.concatenate(ps, axis=0), v_of(hh))
        acc_scr[2 * hh] = alphas[0] * acc_scr[2 * hh] + pv[:TQ]
        acc_scr[2 * hh + 1] = alphas[1] * acc_scr[2 * hh + 1] + pv[TQ:]


def _layer_b_sample_kernel(h1_ref, ck_ref, cv_ref, kn_ref, vn_ref, mk_ref, mv_ref, nb_ref, win_ref, lp_ref, sub_ref,
                           wout_ref, nf_ref, y_ref, h_scr, q_scr, m_scr, l_scr, acc_scr, cat_scr, *, TK, lam_init):
    kt = pl.program_id(1)

    @pl.when(kt == 0)
    def _():
        _diff_project_q(h1_ref[0], nb_ref, win_ref, h_scr, q_scr)
        m_scr[...] = jnp.full(m_scr.shape, NEG_BIG, F32)
        l_scr[...] = jnp.zeros(l_scr.shape, F32)
        acc_scr[...] = jnp.zeros(acc_scr.shape, F32)

    def ck_of(hh, c):
        return ck_ref[pl.ds(2 * hh + c, TK, stride=KV_GROUPS), :].astype(BF16)

    def cv_of(hh):
        return jnp.concatenate([cv_ref[pl.ds(_v_row(hh, dt), TK, stride=KV_GROUPS), :] for dt in range(2)],
                               axis=1).astype(BF16)

    _online_tile(q_scr, ck_of, cv_of, m_scr, l_scr, acc_scr)

    @pl.when(kt == pl.num_programs(1) - 1)
    def _():
        _online_tile(q_scr,
                     lambda hh, c: kn_ref[0, :, hh * HEAD_W + c * DQK: hh * HEAD_W + (c + 1) * DQK],
                     lambda hh: vn_ref[0, :, hh * HEAD_W:(hh + 1) * HEAD_W],
                     m_scr, l_scr, acc_scr)
        lam = _diff_lambda(lp_ref, lam_init)
        o_of = lambda hh: (acc_scr[2 * hh] / l_scr[2 * hh]
                           - lam * (acc_scr[2 * hh + 1] / l_scr[2 * hh + 1]))
        _diff_finish(h1_ref[0], o_of, lam_init, h_scr, win_ref, sub_ref, mk_ref, mv_ref, wout_ref, nf_ref,
                     cat_scr, y_ref)


def _layer_b_sample(h1, ck_rows, cv_rows, kb, vb, mkb, mvb, norm_b, w_in, lam_p, subln, w_out, norm_f, *, P, TK,
                    lam_init):
    B, T, D = h1.shape
    KW = kb.shape[-1]
    M = mkb.shape[1]
    nkt = P // TK
    b0 = lambda b, k: (b, 0, 0)
    cache_spec = pl.BlockSpec((TK * KV_GROUPS, LANES), lambda b, k: (b * nkt + k, 0))
    f = pl.pallas_call(
        functools.partial(_layer_b_sample_kernel, TK=TK, lam_init=lam_init),
        grid=(B, nkt),
        in_specs=[pl.BlockSpec((1, T, D), b0), cache_spec, cache_spec,
                  pl.BlockSpec((1, T, KW), b0), pl.BlockSpec((1, T, KW), b0),
                  pl.BlockSpec((1, M, MEM_W), b0), pl.BlockSpec((1, M, MEM_W), b0),
                  _const_spec((1, D)), _const_spec(w_in.shape), _const_spec(lam_p.shape),
                  _const_spec((1, HEAD_W)), _const_spec(w_out.shape), _const_spec((1, D))],
        out_specs=pl.BlockSpec((1, T, D), b0),
        out_shape=jax.ShapeDtypeStruct((B, T, D), F32),
        scratch_shapes=[pltpu.VMEM((T, D), BF16), pltpu.VMEM((N_HEADS, T, HEAD_W), BF16),
                        pltpu.VMEM((2 * N_HEADS, T, 1), F32), pltpu.VMEM((2 * N_HEADS, T, 1), F32),
                        pltpu.VMEM((2 * N_HEADS, T, HEAD_W), F32),
                        pltpu.VMEM((T, N_HEADS * HEAD_W + MEM_W), BF16)],
        compiler_params=pltpu.CompilerParams(
            dimension_semantics=("arbitrary", "arbitrary"), vmem_limit_bytes=VMEM_LIMIT_BYTES),
        name="layer_b_sample",
    )
    return f(h1, ck_rows, cv_rows, kb, vb, mkb, mvb, norm_b.reshape(1, D), w_in, lam_p, subln.reshape(1, HEAD_W),
             w_out, norm_f.reshape(1, D))


def _rope_tables(pos):
    half = DQK
    inv = ROPE_BASE ** (-jnp.arange(half, dtype=F32) / half)
    ang = pos.astype(F32)[:, None] * inv[None, :]
    return jnp.cos(ang), jnp.sin(ang)


def _pick(n, candidates):
    for c in candidates:
        if n % c == 0:
            return c
    return n


def kernel(x_prompt, x_sample, state_ret, cache_k, cache_v, cache_mem_k, cache_mem_v, mem_prompt,
           norm_a, w_in_a, w_out_a, norm_kv, w_kv, norm_b, w_in_b, diff_lambda, subln_b, w_out_b,
           norm_mem, w_mem_kv, norm_f):
    depth = norm_mem.shape[0]
    assert norm_a.shape[0] == 1 and norm_b.shape[0] == 1 and depth == 2, "kernel is written for depth 2"
    lam_init = 0.8 - 0.6 * math.exp(-0.3 * 1)
    wia, woa, wkv = w_in_a[0].astype(BF16), w_out_a[0].astype(BF16), w_kv.astype(BF16)
    wib, wob = w_in_b[0].astype(BF16), w_out_b[0].astype(BF16)

    Bp, Tp, D = x_prompt.shape
    mk_p, mv_p, mkb_p, mvb_p = _memory_kv(mem_prompt, norm_mem, w_mem_kv)
    cos_p, sin_p = _rope_tables(jnp.arange(Tp))
    blk = _pick(Tp, (256, 128, CHUNK))
    tt = _pick(Tp, (2 * blk, blk))
    h1_p, k_p, v_p, kb_p, vb_p, S_p = _layer_a(
        x_prompt, cos_p, sin_p, None, mkb_p[0], mvb_p[0], norm_a[0], wia, woa, norm_kv, wkv, NB=1, TT=tt, L=blk)
    y_p = _layer_b_prompt(h1_p, kb_p, vb_p, mkb_p[1], mvb_p[1], norm_b[0], wib, diff_lambda[0], subln_b[0], wob,
                          norm_f, TQ=tt, SUB=blk, lam_init=lam_init)

    Bs, Ts, _ = x_sample.shape
    past = cache_k.shape[1]
    nbs = _pick(Bs, (4, 2, 1))
    cos_s, sin_s = _rope_tables(past + jnp.arange(Ts))
    cos_s, sin_s = jnp.tile(cos_s, (nbs, 1)), jnp.tile(sin_s, (nbs, 1))
    M = cache_mem_k.shape[2]
    cmk = cache_mem_k.reshape(depth, Bs, M, MEM_W).astype(BF16)
    cmv = cache_mem_v.reshape(depth, Bs, M, MEM_W).astype(BF16)
    h1_s, k_s, v_s, kb_s, vb_s, S_s = _layer_a(
        x_sample, cos_s, sin_s, state_ret[0], cmk[0], cmv[0], norm_a[0], wia, woa, norm_kv, wkv,
        NB=nbs, TT=Ts, L=Ts)
    ck_rows = cache_k.reshape(Bs * past * KV_GROUPS, LANES)
    cv_rows = cache_v.reshape(Bs, past, N_HEADS, 2, LANES).transpose(0, 1, 3, 2, 4).reshape(
        Bs * past * KV_GROUPS, LANES)
    y_s = _layer_b_sample(h1_s, ck_rows, cv_rows, kb_s, vb_s, cmk[1], cmv[1], norm_b[0], wib, diff_lambda[0],
                          subln_b[0], wob, norm_f, P=past, TK=_pick(past, (512, 256, 128)), lam_init=lam_init)

    return (y_p, y_s, S_p[None], _k_view_to_5d(k_p, Bp, Tp), _v_view_to_4d(v_p, Bp, Tp),
            mk_p, mv_p, S_s[None], _k_view_to_5d(k_s, Bs, Ts), _v_view_to_4d(v_s, Bs, Ts))
```

```python
import functools
import math

import jax
import jax.numpy as jnp
from jax import lax
from jax.experimental import pallas as pl
from jax.experimental.pallas import tpu as pltpu

F32 = jnp.float32
BF16 = jnp.bfloat16

EPS = 1e-6
ROPE_BASE = 10000.0
CHUNK = 64
N_HEADS = 4
HEAD_W = 256
DQK = 128
MEM_HD = 128
MEM_W = N_HEADS * MEM_HD
LANES = 128
KV_GROUPS = N_HEADS * HEAD_W // LANES
NEG_BIG = -1e30
LOG2E = math.log2(math.e)

VMEM_LIMIT_BYTES = 56 * 1024 * 1024


def _dot(a, b):
    return jnp.dot(a, b, preferred_element_type=F32)


def _dot_nt(a, b):
    return lax.dot_general(a, b, (((1,), (1,)), ((), ())), preferred_element_type=F32)


def _dot_tn(a, b):
    return lax.dot_general(a, b, (((0,), (0,)), ((), ())), preferred_element_type=F32)


def _rms(x):
    return x * lax.rsqrt(jnp.mean(x * x, axis=-1, keepdims=True) + EPS)


def _silu(g):
    return g / (1.0 + jnp.exp(-g))


def _mem_spec(arr, layer, nb):
    return pl.BlockSpec((None, nb) + arr.shape[2:], lambda b, t: (layer, b, 0, 0))


def _const_spec(shape):
    nd = len(shape)
    return pl.BlockSpec(shape, lambda *_: (0,) * nd, pipeline_mode=pl.Buffered(1))


def _v_row(h, dt):
    return dt * N_HEADS + h


def _mem_attention(qm, gm, mk_ref, mv_ref, nb, rows, cat_scr, col0):
    row_view = mk_ref.shape[-1] == MEM_HD
    for hh in range(N_HEADS):
        cs = slice(hh * MEM_HD, (hh + 1) * MEM_HD)
        if row_view:
            hsel = pl.ds(hh, mk_ref.shape[1] // N_HEADS, stride=N_HEADS)
            mk, mv = mk_ref[nb, hsel, :].astype(BF16), mv_ref[nb, hsel, :].astype(BF16)
        else:
            mk, mv = mk_ref[nb, :, cs], mv_ref[nb, :, cs]
        q = (qm[rows, cs] * (MEM_HD ** -0.5 * LOG2E)).astype(BF16)
        s = _dot_nt(q, mk)
        p = jnp.exp2(s - jnp.max(s, axis=-1, keepdims=True))
        l = jnp.sum(p, axis=-1, keepdims=True)
        o = _dot(p.astype(BF16), mv) / l
        cat_scr[rows, col0 + hh * MEM_HD: col0 + (hh + 1) * MEM_HD] = (o * _silu(gm[rows, cs])).astype(BF16)


def _memkv_kernel(mem_ref, g_ref, w_ref, mk_ref, mv_ref, mkb_ref, mvb_ref):
    tm = mem_ref.shape[0]
    hb = (_rms(mem_ref[...]) * g_ref[0]).astype(BF16)
    kv = _dot(hb, w_ref[0])
    mk, mv = kv[:, :MEM_W], kv[:, MEM_W:]
    for hh in range(N_HEADS):
        cs = slice(hh * MEM_HD, (hh + 1) * MEM_HD)
        mk_ref[0, pl.ds(hh, tm, stride=N_HEADS), :] = mk[:, cs]
        mv_ref[0, pl.ds(hh, tm, stride=N_HEADS), :] = mv[:, cs]
    mkb_ref[0] = mk.astype(BF16)
    mvb_ref[0] = mv.astype(BF16)


def _memory_kv(mem, norm_mem, w_mem_kv):
    B, M, D = mem.shape
    nl = norm_mem.shape[0]
    rows = B * M
    tm = 512 if rows % 512 == 0 else M
    f = pl.pallas_call(
        _memkv_kernel,
        grid=(nl, rows // tm),
        in_specs=[
            pl.BlockSpec((tm, D), lambda l, i: (i, 0)),
            pl.BlockSpec((1, 1, D), lambda l, i: (l, 0, 0)),
            pl.BlockSpec((1, D, 2 * MEM_W), lambda l, i: (l, 0, 0)),
        ],
        out_specs=[pl.BlockSpec((1, tm * N_HEADS, MEM_HD), lambda l, i: (l, i, 0))] * 2
        + [pl.BlockSpec((1, tm, MEM_W), lambda l, i: (l, i, 0))] * 2,
        out_shape=[jax.ShapeDtypeStruct((nl, rows * N_HEADS, MEM_HD), F32)] * 2
        + [jax.ShapeDtypeStruct((nl, rows, MEM_W), BF16)] * 2,
        compiler_params=pltpu.CompilerParams(
            dimension_semantics=("arbitrary", "arbitrary"), vmem_limit_bytes=VMEM_LIMIT_BYTES),
        name="memkv",
    )
    mk, mv, mkb, mvb = f(mem.reshape(rows, D), norm_mem.reshape(nl, 1, D), w_mem_kv.astype(BF16))
    shp5 = (nl, B, M, N_HEADS, MEM_HD)
    return mk.reshape(shp5), mv.reshape(shp5), mkb.reshape(nl, B, M, MEM_W), mvb.reshape(nl, B, M, MEM_W)


def _layer_a_kernel(*refs, NB, TT, L, has_state):
    it = iter(refs)
    x_ref, cos_ref, sin_ref = next(it), next(it), next(it)
    s0_ref = next(it) if has_state else None
    mk_ref, mv_ref, na_ref, win_ref, wout_ref, nkv_ref, wkv_ref = (next(it) for _ in range(7))
    h1_ref, k_ref, v_ref, kb_ref, vb_ref, S_ref = (next(it) for _ in range(6))
    h_scr, cat_scr, dec_scr = (next(it) for _ in range(3))

    R = NB * TT
    D = x_ref.shape[-1]
    QW = N_HEADS * HEAD_W
    t = pl.program_id(1)
    log_g = [math.log(1.0 - 2.0 ** (-5.0 - h)) for h in range(N_HEADS)]

    @pl.when((pl.program_id(0) == 0) & (t == 0))
    def _():
        d = (lax.broadcasted_iota(jnp.int32, (L, L), 0) - lax.broadcasted_iota(jnp.int32, (L, L), 1)).astype(F32)
        for h in range(N_HEADS):
            dec_scr[h] = jnp.where(d >= 0, jnp.exp(log_g[h] * jnp.maximum(d, 0.0)), 0.0)

    @pl.when(t == 0)
    def _():
        if has_state:
            S_ref[...] = s0_ref[...]
        else:
            S_ref[...] = jnp.zeros(S_ref.shape, F32)

    x = x_ref[...].reshape(R, D)
    h_scr[...] = (_rms(x) * na_ref[...]).astype(BF16)
    cos = cos_ref[...]
    sin = sin_ref[...]
    half = HEAD_W // 2
    idx = lax.broadcasted_iota(jnp.int32, (L, 1), 0).astype(F32)

    def rope(c0):
        y = _dot(h_scr[...], win_ref[:, c0:c0 + HEAD_W])
        y1, y2 = y[:, :half], y[:, half:]
        return jnp.concatenate([y1 * cos - y2 * sin, y2 * cos + y1 * sin], axis=-1)

    for h in range(N_HEADS):
        q = rope(h * HEAD_W).astype(BF16)
        k = rope(QW + h * HEAD_W) * (HEAD_W ** -0.5)
        v = _dot(h_scr[...], win_ref[:, 2 * QW + h * HEAD_W: 2 * QW + (h + 1) * HEAD_W]).astype(BF16)
        g = _dot(h_scr[...], win_ref[:, 3 * QW + h * HEAD_W: 3 * QW + (h + 1) * HEAD_W])
        a_in = jnp.exp(log_g[h] * (idx + 1.0))
        b_out = jnp.exp(log_g[h] * (L - 1.0 - idx))
        g_blk = math.exp(log_g[h] * L)
        for nb in range(NB):
            for j in range(TT // L):
                rows = slice(nb * TT + j * L, nb * TT + (j + 1) * L)
                qb, vb = q[rows], v[rows]
                kf = k[rows]
                S = S_ref[nb, h]
                sc = _dot_nt(qb, kf.astype(BF16)) * dec_scr[h]
                o = _dot(sc.astype(BF16), vb) + _dot(qb, S.astype(BF16)) * a_in
                S_ref[nb, h] = g_blk * S + _dot_tn((kf * b_out).astype(BF16), vb)
                cat_scr[rows, h * HEAD_W:(h + 1) * HEAD_W] = (_rms(o) * _silu(g[rows])).astype(BF16)

    qm = _dot(h_scr[...], win_ref[:, 4 * QW: 4 * QW + MEM_W])
    gm = _dot(h_scr[...], win_ref[:, 4 * QW + MEM_W: 4 * QW + 2 * MEM_W])
    for nb in range(NB):
        _mem_attention(qm, gm, mk_ref, mv_ref, nb, slice(nb * TT, (nb + 1) * TT), cat_scr, QW)

    NC = 512
    for c in range(D // NC):
        cs = slice(c * NC, (c + 1) * NC)
        y = _dot(cat_scr[...], wout_ref[:, cs]) + x_ref[:, :, cs].reshape(R, NC)
        h1_ref[:, :, cs] = y.reshape(NB, TT, NC)

    hk = (_rms(h1_ref[...].reshape(R, D)) * nkv_ref[...]).astype(BF16)
    KW = N_HEADS * HEAD_W
    for c in range(2 * KW // NC):
        kv = _dot(hk, wkv_ref[:, c * NC:(c + 1) * NC])
        is_k = c < KW // NC
        col0 = c * NC if is_k else c * NC - KW
        for hl in range(NC // HEAD_W):
            (kb_ref if is_k else vb_ref)[:, col0 // HEAD_W + hl] = (
                kv[:, hl * HEAD_W:(hl + 1) * HEAD_W].astype(BF16).reshape(NB, TT, HEAD_W))
        for jj in range(NC // LANES):
            col = col0 + jj * LANES
            if is_k:
                row = col // LANES
                k_ref[pl.ds(row, R, stride=KV_GROUPS), :] = kv[:, jj * LANES:(jj + 1) * LANES]
            else:
                row = _v_row(col // HEAD_W, (col % HEAD_W) // LANES)
                v_ref[pl.ds(row, R, stride=KV_GROUPS), :] = kv[:, jj * LANES:(jj + 1) * LANES]


def _layer_a(x, cos, sin, s0, mkb, mvb, norm_a, w_in, w_out, norm_kv, w_kv, *, NB, TT, L, mem_layer):
    B, T, D = x.shape
    R = NB * TT
    nt = T // TT
    assert NB == 1 or nt == 1
    KW = w_kv.shape[1] // 2
    has_state = s0 is not None
    bt = lambda b, t: (b, t, 0)
    b0 = lambda b, t: (b, 0, 0)
    in_specs = [pl.BlockSpec((NB, TT, D), bt),
                pl.BlockSpec((R, DQK), lambda b, t: (t, 0)),
                pl.BlockSpec((R, DQK), lambda b, t: (t, 0))]
    args = [x, cos, sin]
    if has_state:
        in_specs.append(pl.BlockSpec((NB, N_HEADS, HEAD_W, HEAD_W), lambda b, t: (b, 0, 0, 0)))
        args.append(s0)
    in_specs += [_mem_spec(mkb, mem_layer, NB), _mem_spec(mvb, mem_layer, NB),
                 _const_spec((1, D)), _const_spec(w_in.shape), _const_spec(w_out.shape),
                 _const_spec((1, D)), _const_spec(w_kv.shape)]
    args += [mkb, mvb, norm_a.reshape(1, D), w_in, w_out, norm_kv.reshape(1, D), w_kv]
    rows_spec = pl.BlockSpec((R * KV_GROUPS, LANES), lambda b, t: (b * nt + t, 0))
    assert KW == N_HEADS * HEAD_W
    hm_spec = pl.BlockSpec((NB, N_HEADS, TT, HEAD_W), lambda b, t: (b, 0, t, 0))
    out_specs = [pl.BlockSpec((NB, TT, D), bt), rows_spec, rows_spec, hm_spec, hm_spec,
                 pl.BlockSpec((NB, N_HEADS, HEAD_W, HEAD_W), lambda b, t: (b, 0, 0, 0))]
    out_shape = [jax.ShapeDtypeStruct((B, T, D), F32)] + [
        jax.ShapeDtypeStruct((B * T * KV_GROUPS, LANES), F32)] * 2 + [
        jax.ShapeDtypeStruct((B, N_HEADS, T, HEAD_W), BF16)] * 2 + [
        jax.ShapeDtypeStruct((B, N_HEADS, HEAD_W, HEAD_W), F32)]
    f = pl.pallas_call(
        functools.partial(_layer_a_kernel, NB=NB, TT=TT, L=L, has_state=has_state),
        grid=(B // NB, nt),
        in_specs=in_specs,
        out_specs=out_specs,
        out_shape=out_shape,
        scratch_shapes=[pltpu.VMEM((R, D), BF16), pltpu.VMEM((R, N_HEADS * HEAD_W + MEM_W), BF16),
                        pltpu.VMEM((N_HEADS, L, L), F32)],
        compiler_params=pltpu.CompilerParams(
            dimension_semantics=("arbitrary", "arbitrary"), vmem_limit_bytes=VMEM_LIMIT_BYTES),
        name="layer_a",
    )
    return f(*args)


def _k_view_to_5d(kr, B, T):
    return kr.reshape(B, T, N_HEADS, 2, DQK)


def _v_view_to_4d(vr, B, T):
    return vr.reshape(B, T, 2, N_HEADS, LANES).transpose(0, 1, 3, 2, 4).reshape(B, T, N_HEADS, HEAD_W)


def _diff_lambda(lp_ref, lam_init):
    lp = lp_ref[...]
    a = jnp.sum(lp[0:1] * lp[1:2], axis=-1, keepdims=True)
    b = jnp.sum(lp[2:3] * lp[3:4], axis=-1, keepdims=True)
    return jnp.exp(a) - jnp.exp(b) + lam_init


def _diff_project_q(h1, nb_ref, win_ref, h_scr, q_scr):
    h_scr[...] = (_rms(h1) * nb_ref[...]).astype(BF16)
    for hh in range(N_HEADS):
        q = _dot(h_scr[...], win_ref[:, hh * HEAD_W:(hh + 1) * HEAD_W]) * (DQK ** -0.5 * LOG2E)
        q_scr[hh] = q.astype(BF16)


def _diff_finish(h1, o_of, lam_init, h_scr, win_ref, sub_ref, mk_ref, mv_ref, wout_ref, nf_ref, cat_scr, y_ref):
    TQ, D = h1.shape
    QW = N_HEADS * HEAD_W
    for hh in range(N_HEADS):
        o = _rms(o_of(hh)) * sub_ref[...] * (1.0 - lam_init)
        g = _dot(h_scr[...], win_ref[:, QW + hh * HEAD_W: QW + (hh + 1) * HEAD_W])
        cat_scr[:, hh * HEAD_W:(hh + 1) * HEAD_W] = (o * _silu(g)).astype(BF16)
    qm = _dot(h_scr[...], win_ref[:, 2 * QW: 2 * QW + MEM_W])
    gm = _dot(h_scr[...], win_ref[:, 2 * QW + MEM_W: 2 * QW + 2 * MEM_W])
    _mem_attention(qm, gm, mk_ref, mv_ref, 0, slice(0, TQ), cat_scr, QW)
    NC = 512
    for c in range(D // NC):
        cs = slice(c * NC, (c + 1) * NC)
        y_ref[0, :, cs] = _dot(cat_scr[...], wout_ref[:, cs]) + h1[:, cs]
    y_ref[0] = _rms(y_ref[0]) * nf_ref[...]


def _layer_b_prompt_kernel(h1_ref, kb_ref, vb_ref, mk_ref, mv_ref, nb_ref, win_ref, lp_ref, sub_ref, wout_ref,
                           nf_ref, y_ref, h_scr, q_scr, o_scr, cat_scr, *, TQ, SUB, NQ, lam_init):
    qi = pl.program_id(1)
    h1 = h1_ref[0]
    _diff_project_q(h1, nb_ref, win_ref, h_scr, q_scr)
    lam = _diff_lambda(lp_ref, lam_init)
    diag = (lax.broadcasted_iota(jnp.int32, (SUB, SUB), 1) // CHUNK) <= (
        lax.broadcasted_iota(jnp.int32, (SUB, SUB), 0) // CHUNK)

    for case in range(NQ):
        @pl.when(qi == case)
        def _(case=case):
            for sub in range(TQ // SUB):
                rows = slice(sub * SUB, (sub + 1) * SUB)
                kt = case * TQ + (sub + 1) * SUB
                for hh in range(N_HEADS):
                    ps, ls = [], []
                    for c in range(2):
                        s = _dot_nt(q_scr[hh, rows, c * DQK:(c + 1) * DQK],
                                    kb_ref[0, hh, :kt, c * DQK:(c + 1) * DQK])
                        last = jnp.where(diag, s[:, kt - SUB:], NEG_BIG)
                        s = last if kt == SUB else jnp.concatenate([s[:, :kt - SUB], last], axis=1)
                        p = jnp.exp2(s - jnp.max(s, axis=-1, keepdims=True))
                        ls.append(jnp.sum(p, axis=-1, keepdims=True))
                        ps.append(p.astype(BF16))
                    pv = _dot(jnp.concatenate(ps, axis=0), vb_ref[0, hh, :kt, :])
                    o_scr[hh, rows] = pv[:SUB] / ls[0] - lam * (pv[SUB:] / ls[1])

    _diff_finish(h1, lambda hh: o_scr[hh], lam_init, h_scr, win_ref, sub_ref, mk_ref, mv_ref, wout_ref, nf_ref,
                 cat_scr, y_ref)


def _layer_b_prompt(h1, kb, vb, mkb, mvb, norm_b, w_in, lam_p, subln, w_out, norm_f, *, TQ, SUB, lam_init,
                    mem_layer):
    B, T, D = h1.shape
    kv_spec = pl.BlockSpec((1, N_HEADS, T, HEAD_W), lambda b, q: (b, 0, 0, 0))
    f = pl.pallas_call(
        functools.partial(_layer_b_prompt_kernel, TQ=TQ, SUB=SUB, NQ=T // TQ, lam_init=lam_init),
        grid=(B, T // TQ),
        in_specs=[pl.BlockSpec((1, TQ, D), lambda b, q: (b, q, 0)), kv_spec, kv_spec,
                  _mem_spec(mkb, mem_layer, 1), _mem_spec(mvb, mem_layer, 1),
                  _const_spec((1, D)), _const_spec(w_in.shape), _const_spec(lam_p.shape),
                  _const_spec((1, HEAD_W)), _const_spec(w_out.shape), _const_spec((1, D))],
        out_specs=pl.BlockSpec((1, TQ, D), lambda b, q: (b, q, 0)),
        out_shape=jax.ShapeDtypeStruct((B, T, D), F32),
        scratch_shapes=[pltpu.VMEM((TQ, D), BF16), pltpu.VMEM((N_HEADS, TQ, HEAD_W), BF16),
                        pltpu.VMEM((N_HEADS, TQ, HEAD_W), F32),
                        pltpu.VMEM((TQ, N_HEADS * HEAD_W + MEM_W), BF16)],
        compiler_params=pltpu.CompilerParams(
            dimension_semantics=("arbitrary", "arbitrary"), vmem_limit_bytes=VMEM_LIMIT_BYTES),
        name="layer_b_prompt",
    )
    return f(h1, kb, vb, mkb, mvb, norm_b.reshape(1, D), w_in, lam_p, subln.reshape(1, HEAD_W), w_out,
             norm_f.reshape(1, D))


def _online_tile(q_scr, k_of, v_of, m_scr, l_scr, acc_scr):
    TQ = q_scr.shape[1]
    for hh in range(N_HEADS):
        ps, alphas = [], []
        for c in range(2):
            i = 2 * hh + c
            s = _dot_nt(q_scr[hh, :, c * DQK:(c + 1) * DQK], k_of(hh, c))
            m_prev = m_scr[i]
            m_new = jnp.maximum(m_prev, jnp.max(s, axis=-1, keepdims=True))
            alpha = jnp.exp2(m_prev - m_new)
            p = jnp.exp2(s - m_new)
            l_scr[i] = alpha * l_scr[i] + jnp.sum(p, axis=-1, keepdims=True)
            m_scr[i] = m_new
            ps.append(p.astype(BF16))
            alphas.append(alpha)
        pv = _dot(jnp.concatenate(ps, axis=0), v_of(hh))
        acc_scr[2 * hh] = alphas[0] * acc_scr[2 * hh] + pv[:TQ]
        acc_scr[2 * hh + 1] = alphas[1] * acc_scr[2 * hh + 1] + pv[TQ:]


def _layer_b_sample_kernel(h1_ref, ck_ref, cv_ref, kn_ref, vn_ref, mk_ref, mv_ref, nb_ref, win_ref, lp_ref, sub_ref,
                           wout_ref, nf_ref, y_ref, h_scr, q_scr, m_scr, l_scr, acc_scr, cat_scr, *, TK, lam_init):
    kt = pl.program_id(1)

    @pl.when(kt == 0)
    def _():
        _diff_project_q(h1_ref[0], nb_ref, win_ref, h_scr, q_scr)
        m_scr[...] = jnp.full(m_scr.shape, NEG_BIG, F32)
        l_scr[...] = jnp.zeros(l_scr.shape, F32)
        acc_scr[...] = jnp.zeros(acc_scr.shape, F32)

    def ck_of(hh, c):
        return ck_ref[pl.ds(2 * hh + c, TK, stride=KV_GROUPS), :].astype(BF16)

    def cv_of(hh):
        return jnp.concatenate([cv_ref[pl.ds(_v_row(hh, dt), TK, stride=KV_GROUPS), :] for dt in range(2)],
                               axis=1).astype(BF16)

    _online_tile(q_scr, ck_of, cv_of, m_scr, l_scr, acc_scr)

    @pl.when(kt == pl.num_programs(1) - 1)
    def _():
        _online_tile(q_scr,
                     lambda hh, c: kn_ref[0, hh, :, c * DQK:(c + 1) * DQK],
                     lambda hh: vn_ref[0, hh],
                     m_scr, l_scr, acc_scr)
        lam = _diff_lambda(lp_ref, lam_init)
        o_of = lambda hh: (acc_scr[2 * hh] / l_scr[2 * hh]
                           - lam * (acc_scr[2 * hh + 1] / l_scr[2 * hh + 1]))
        _diff_finish(h1_ref[0], o_of, lam_init, h_scr, win_ref, sub_ref, mk_ref, mv_ref, wout_ref, nf_ref,
                     cat_scr, y_ref)


def _layer_b_sample(h1, ck_rows, cv_rows, kb, vb, mkb, mvb, norm_b, w_in, lam_p, subln, w_out, norm_f, *, P, TK,
                    lam_init, mem_layer):
    B, T, D = h1.shape
    nkt = P // TK
    b0 = lambda b, k: (b, 0, 0)
    cache_spec = pl.BlockSpec((TK * KV_GROUPS, LANES), lambda b, k: (b * nkt + k, 0))
    new_spec = pl.BlockSpec((1, N_HEADS, T, HEAD_W), lambda b, k: (b, 0, 0, 0))
    f = pl.pallas_call(
        functools.partial(_layer_b_sample_kernel, TK=TK, lam_init=lam_init),
        grid=(B, nkt),
        in_specs=[pl.BlockSpec((1, T, D), b0), cache_spec, cache_spec, new_spec, new_spec,
                  _mem_spec(mkb, mem_layer, 1), _mem_spec(mvb, mem_layer, 1),
                  _const_spec((1, D)), _const_spec(w_in.shape), _const_spec(lam_p.shape),
                  _const_spec((1, HEAD_W)), _const_spec(w_out.shape), _const_spec((1, D))],
        out_specs=pl.BlockSpec((1, T, D), b0),
        out_shape=jax.ShapeDtypeStruct((B, T, D), F32),
        scratch_shapes=[pltpu.VMEM((T, D), BF16), pltpu.VMEM((N_HEADS, T, HEAD_W), BF16),
                        pltpu.VMEM((2 * N_HEADS, T, 1), F32), pltpu.VMEM((2 * N_HEADS, T, 1), F32),
                        pltpu.VMEM((2 * N_HEADS, T, HEAD_W), F32),
                        pltpu.VMEM((T, N_HEADS * HEAD_W + MEM_W), BF16)],
        compiler_params=pltpu.CompilerParams(
            dimension_semantics=("arbitrary", "arbitrary"), vmem_limit_bytes=VMEM_LIMIT_BYTES),
        name="layer_b_sample",
    )
    return f(h1, ck_rows, cv_rows, kb, vb, mkb, mvb, norm_b.reshape(1, D), w_in, lam_p, subln.reshape(1, HEAD_W),
             w_out, norm_f.reshape(1, D))


def _rope_tables(pos):
    half = DQK
    inv = ROPE_BASE ** (-jnp.arange(half, dtype=F32) / half)
    ang = pos.astype(F32)[:, None] * inv[None, :]
    return jnp.cos(ang), jnp.sin(ang)


def _pick(n, candidates):
    for c in candidates:
        if n % c == 0:
            return c
    return n


def kernel(x_prompt, x_sample, state_ret, cache_k, cache_v, cache_mem_k, cache_mem_v, mem_prompt,
           norm_a, w_in_a, w_out_a, norm_kv, w_kv, norm_b, w_in_b, diff_lambda, subln_b, w_out_b,
           norm_mem, w_mem_kv, norm_f):
    depth = norm_mem.shape[0]
    assert norm_a.shape[0] == 1 and norm_b.shape[0] == 1 and depth == 2, "kernel is written for depth 2"
    lam_init = 0.8 - 0.6 * math.exp(-0.3 * 1)
    wia, woa, wkv = w_in_a[0].astype(BF16), w_out_a[0].astype(BF16), w_kv.astype(BF16)
    wib, wob = w_in_b[0].astype(BF16), w_out_b[0].astype(BF16)

    Bp, Tp, D = x_prompt.shape
    mk_p, mv_p, mkb_p, mvb_p = _memory_kv(mem_prompt, norm_mem, w_mem_kv)
    cos_p, sin_p = _rope_tables(jnp.arange(Tp))
    blk = _pick(Tp, (256, 128, CHUNK))
    tt = _pick(Tp, (2 * blk, blk))
    h1_p, k_p, v_p, kb_p, vb_p, S_p = _layer_a(
        x_prompt, cos_p, sin_p, None, mkb_p, mvb_p, norm_a[0], wia, woa, norm_kv, wkv, NB=1, TT=tt, L=blk,
        mem_layer=0)
    y_p = _layer_b_prompt(h1_p, kb_p, vb_p, mkb_p, mvb_p, norm_b[0], wib, diff_lambda[0], subln_b[0], wob,
                          norm_f, TQ=blk, SUB=blk, lam_init=lam_init, mem_layer=1)

    Bs, Ts, _ = x_sample.shape
    past = cache_k.shape[1]
    nbs = _pick(Bs, (4, 2, 1))
    cos_s, sin_s = _rope_tables(past + jnp.arange(Ts))
    cos_s, sin_s = jnp.tile(cos_s, (nbs, 1)), jnp.tile(sin_s, (nbs, 1))
    M = cache_mem_k.shape[2]
    cmk = cache_mem_k.reshape(depth, Bs, M * N_HEADS, MEM_HD)
    cmv = cache_mem_v.reshape(depth, Bs, M * N_HEADS, MEM_HD)
    h1_s, k_s, v_s, kb_s, vb_s, S_s = _layer_a(
        x_sample, cos_s, sin_s, state_ret[0], cmk, cmv, norm_a[0], wia, woa, norm_kv, wkv,
        NB=nbs, TT=Ts, L=Ts, mem_layer=0)
    ck_rows = cache_k.reshape(Bs * past * KV_GROUPS, LANES)
    cv_rows = cache_v.reshape(Bs, past, N_HEADS, 2, LANES).transpose(0, 1, 3, 2, 4).reshape(
        Bs * past * KV_GROUPS, LANES)
    y_s = _layer_b_sample(h1_s, ck_rows, cv_rows, kb_s, vb_s, cmk, cmv, norm_b[0], wib, diff_lambda[0],
                          subln_b[0], wob, norm_f, P=past, TK=_pick(past, (1024, 512, 256, 128)), lam_init=lam_init,
                          mem_layer=1)

    return (y_p, y_s, S_p[None], _k_view_to_5d(k_p, Bp, Tp), _v_view_to_4d(v_p, Bp, Tp),
            mk_p, mv_p, S_s[None], _k_view_to_5d(k_s, Bs, Ts), _v_view_to_4d(v_s, Bs, Ts))
```

```python
import functools
import math

import jax
import jax.numpy as jnp
from jax import lax
from jax.experimental import pallas as pl
from jax.experimental.pallas import tpu as pltpu

F32 = jnp.float32
BF16 = jnp.bfloat16

EPS = 1e-6
ROPE_BASE = 10000.0
CHUNK = 64
N_HEADS = 4
HEAD_W = 256
DQK = 128
MEM_HD = 128
MEM_W = N_HEADS * MEM_HD
LANES = 128
KV_GROUPS = N_HEADS * HEAD_W // LANES
NEG_BIG = -1e30
LOG2E = math.log2(math.e)

VMEM_LIMIT_BYTES = 56 * 1024 * 1024


def _dot(a, b):
    return jnp.dot(a, b, preferred_element_type=F32)


def _dot_nt(a, b):
    return lax.dot_general(a, b, (((1,), (1,)), ((), ())), preferred_element_type=F32)


def _dot_tn(a, b):
    return lax.dot_general(a, b, (((0,), (0,)), ((), ())), preferred_element_type=F32)


def _rms(x):
    return x * lax.rsqrt(jnp.mean(x * x, axis=-1, keepdims=True) + EPS)


def _silu(g):
    return g / (1.0 + jnp.exp(-g))


def _mem_spec(arr, layer, nb):
    return pl.BlockSpec((None, nb) + arr.shape[2:], lambda b, t: (layer, b, 0, 0))


def _const_spec(shape):
    nd = len(shape)
    return pl.BlockSpec(shape, lambda *_: (0,) * nd, pipeline_mode=pl.Buffered(1))


def _v_row(h, dt):
    return dt * N_HEADS + h


def _mem_attention(qm, gm, mk_ref, mv_ref, nb, rows, cat_scr, col0):
    row_view = mk_ref.shape[-1] == MEM_HD
    for hh in range(N_HEADS):
        cs = slice(hh * MEM_HD, (hh + 1) * MEM_HD)
        if row_view:
            hsel = pl.ds(hh, mk_ref.shape[1] // N_HEADS, stride=N_HEADS)
            mk, mv = mk_ref[nb, hsel, :].astype(BF16), mv_ref[nb, hsel, :].astype(BF16)
        else:
            mk, mv = mk_ref[nb, :, cs], mv_ref[nb, :, cs]
        q = (qm[rows, cs] * (MEM_HD ** -0.5 * LOG2E)).astype(BF16)
        s = _dot_nt(q, mk)
        p = jnp.exp2(s - jnp.max(s, axis=-1, keepdims=True))
        l = jnp.sum(p, axis=-1, keepdims=True)
        o = _dot(p.astype(BF16), mv) / l
        cat_scr[rows, col0 + hh * MEM_HD: col0 + (hh + 1) * MEM_HD] = (o * _silu(gm[rows, cs])).astype(BF16)


def _memkv_kernel(mem_ref, g_ref, w_ref, mk_ref, mv_ref, mkb_ref, mvb_ref):
    tm = mem_ref.shape[0]
    hb = (_rms(mem_ref[...]) * g_ref[0]).astype(BF16)
    kv = _dot(hb, w_ref[0])
    mk, mv = kv[:, :MEM_W], kv[:, MEM_W:]
    for hh in range(N_HEADS):
        cs = slice(hh * MEM_HD, (hh + 1) * MEM_HD)
        mk_ref[0, pl.ds(hh, tm, stride=N_HEADS), :] = mk[:, cs]
        mv_ref[0, pl.ds(hh, tm, stride=N_HEADS), :] = mv[:, cs]
    mkb_ref[0] = mk.astype(BF16)
    mvb_ref[0] = mv.astype(BF16)


def _memory_kv(mem, norm_mem, w_mem_kv):
    B, M, D = mem.shape
    nl = norm_mem.shape[0]
    rows = B * M
    tm = 512 if rows % 512 == 0 else M
    f = pl.pallas_call(
        _memkv_kernel,
        grid=(nl, rows // tm),
        in_specs=[
            pl.BlockSpec((tm, D), lambda l, i: (i, 0)),
            pl.BlockSpec((1, 1, D), lambda l, i: (l, 0, 0)),
            pl.BlockSpec((1, D, 2 * MEM_W), lambda l, i: (l, 0, 0)),
        ],
        out_specs=[pl.BlockSpec((1, tm * N_HEADS, MEM_HD), lambda l, i: (l, i, 0))] * 2
        + [pl.BlockSpec((1, tm, MEM_W), lambda l, i: (l, i, 0))] * 2,
        out_shape=[jax.ShapeDtypeStruct((nl, rows * N_HEADS, MEM_HD), F32)] * 2
        + [jax.ShapeDtypeStruct((nl, rows, MEM_W), BF16)] * 2,
        compiler_params=pltpu.CompilerParams(
            dimension_semantics=("arbitrary", "arbitrary"), vmem_limit_bytes=VMEM_LIMIT_BYTES),
        name="memkv",
    )
    mk, mv, mkb, mvb = f(mem.reshape(rows, D), norm_mem.reshape(nl, 1, D), w_mem_kv.astype(BF16))
    shp5 = (nl, B, M, N_HEADS, MEM_HD)
    return mk.reshape(shp5), mv.reshape(shp5), mkb.reshape(nl, B, M, MEM_W), mvb.reshape(nl, B, M, MEM_W)


def _layer_a_kernel(*refs, NB, TT, L, has_state):
    it = iter(refs)
    x_ref, cos_ref, sin_ref = next(it), next(it), next(it)
    s0_ref = next(it) if has_state else None
    mk_ref, mv_ref, na_ref, win_ref, wout_ref, nkv_ref, wkv_ref = (next(it) for _ in range(7))
    h1_ref, k_ref, v_ref, kb_ref, vb_ref, S_ref = (next(it) for _ in range(6))
    h_scr, cat_scr, dec_scr = (next(it) for _ in range(3))

    R = NB * TT
    D = x_ref.shape[-1]
    QW = N_HEADS * HEAD_W
    t = pl.program_id(1)
    log_g = [math.log(1.0 - 2.0 ** (-5.0 - h)) for h in range(N_HEADS)]

    @pl.when((pl.program_id(0) == 0) & (t == 0))
    def _():
        d = (lax.broadcasted_iota(jnp.int32, (L, L), 0) - lax.broadcasted_iota(jnp.int32, (L, L), 1)).astype(F32)
        for h in range(N_HEADS):
            dec_scr[h] = jnp.where(d >= 0, jnp.exp(log_g[h] * jnp.maximum(d, 0.0)), 0.0)

    @pl.when(t == 0)
    def _():
        if has_state:
            S_ref[...] = s0_ref[...]
        else:
            S_ref[...] = jnp.zeros(S_ref.shape, F32)

    x = x_ref[...].reshape(R, D)
    h_scr[...] = (_rms(x) * na_ref[...]).astype(BF16)
    cos = cos_ref[...]
    sin = sin_ref[...]
    half = HEAD_W // 2
    idx = lax.broadcasted_iota(jnp.int32, (L, 1), 0).astype(F32)

    def rope(c0):
        y = _dot(h_scr[...], win_ref[:, c0:c0 + HEAD_W])
        y1, y2 = y[:, :half], y[:, half:]
        return jnp.concatenate([y1 * cos - y2 * sin, y2 * cos + y1 * sin], axis=-1)

    for h in range(N_HEADS):
        q = rope(h * HEAD_W).astype(BF16)
        k = rope(QW + h * HEAD_W) * (HEAD_W ** -0.5)
        v = _dot(h_scr[...], win_ref[:, 2 * QW + h * HEAD_W: 2 * QW + (h + 1) * HEAD_W]).astype(BF16)
        g = _dot(h_scr[...], win_ref[:, 3 * QW + h * HEAD_W: 3 * QW + (h + 1) * HEAD_W])
        a_in = jnp.exp(log_g[h] * (idx + 1.0))
        b_out = jnp.exp(log_g[h] * (L - 1.0 - idx))
        g_blk = math.exp(log_g[h] * L)
        for nb in range(NB):
            for j in range(TT // L):
                rows = slice(nb * TT + j * L, nb * TT + (j + 1) * L)
                qb, vb = q[rows], v[rows]
                kf = k[rows]
                S = S_ref[nb, h]
                sc = _dot_nt(qb, kf.astype(BF16)) * dec_scr[h]
                o = _dot(sc.astype(BF16), vb) + _dot(qb, S.astype(BF16)) * a_in
                S_ref[nb, h] = g_blk * S + _dot_tn((kf * b_out).astype(BF16), vb)
                cat_scr[rows, h * HEAD_W:(h + 1) * HEAD_W] = (_rms(o) * _silu(g[rows])).astype(BF16)

    qm = _dot(h_scr[...], win_ref[:, 4 * QW: 4 * QW + MEM_W])
    gm = _dot(h_scr[...], win_ref[:, 4 * QW + MEM_W: 4 * QW + 2 * MEM_W])
    for nb in range(NB):
        _mem_attention(qm, gm, mk_ref, mv_ref, nb, slice(nb * TT, (nb + 1) * TT), cat_scr, QW)

    NC = 512
    for c in range(D // NC):
        cs = slice(c * NC, (c + 1) * NC)
        y = _dot(cat_scr[...], wout_ref[:, cs]) + x_ref[:, :, cs].reshape(R, NC)
        h1_ref[:, :, cs] = y.reshape(NB, TT, NC)

    hk = (_rms(h1_ref[...].reshape(R, D)) * nkv_ref[...]).astype(BF16)
    KW = N_HEADS * HEAD_W
    for c in range(2 * KW // NC):
        kv = _dot(hk, wkv_ref[:, c * NC:(c + 1) * NC])
        is_k = c < KW // NC
        col0 = c * NC if is_k else c * NC - KW
        for hl in range(NC // HEAD_W):
            (kb_ref if is_k else vb_ref)[:, col0 // HEAD_W + hl] = (
                kv[:, hl * HEAD_W:(hl + 1) * HEAD_W].astype(BF16).reshape(NB, TT, HEAD_W))
        for jj in range(NC // LANES):
            col = col0 + jj * LANES
            if is_k:
                row = col // LANES
                k_ref[pl.ds(row, R, stride=KV_GROUPS), :] = kv[:, jj * LANES:(jj + 1) * LANES]
            else:
                row = _v_row(col // HEAD_W, (col % HEAD_W) // LANES)
                v_ref[pl.ds(row, R, stride=KV_GROUPS), :] = kv[:, jj * LANES:(jj + 1) * LANES]


def _layer_a(x, cos, sin, s0, mkb, mvb, norm_a, w_in, w_out, norm_kv, w_kv, *, NB, TT, L, mem_layer):
    B, T, D = x.shape
    R = NB * TT
    nt = T // TT
    assert NB == 1 or nt == 1
    KW = w_kv.shape[1] // 2
    has_state = s0 is not None
    bt = lambda b, t: (b, t, 0)
    b0 = lambda b, t: (b, 0, 0)
    in_specs = [pl.BlockSpec((NB, TT, D), bt),
                pl.BlockSpec((R, DQK), lambda b, t: (t, 0)),
                pl.BlockSpec((R, DQK), lambda b, t: (t, 0))]
    args = [x, cos, sin]
    if has_state:
        in_specs.append(pl.BlockSpec((NB, N_HEADS, HEAD_W, HEAD_W), lambda b, t: (b, 0, 0, 0)))
        args.append(s0)
    in_specs += [_mem_spec(mkb, mem_layer, NB), _mem_spec(mvb, mem_layer, NB),
                 _const_spec((1, D)), _const_spec(w_in.shape), _const_spec(w_out.shape),
                 _const_spec((1, D)), _const_spec(w_kv.shape)]
    args += [mkb, mvb, norm_a.reshape(1, D), w_in, w_out, norm_kv.reshape(1, D), w_kv]
    rows_spec = pl.BlockSpec((R * KV_GROUPS, LANES), lambda b, t: (b * nt + t, 0))
    assert KW == N_HEADS * HEAD_W
    hm_spec = pl.BlockSpec((NB, N_HEADS, TT, HEAD_W), lambda b, t: (b, 0, t, 0))
    out_specs = [pl.BlockSpec((NB, TT, D), bt), rows_spec, rows_spec, hm_spec, hm_spec,
                 pl.BlockSpec((NB, N_HEADS, HEAD_W, HEAD_W), lambda b, t: (b, 0, 0, 0))]
    out_shape = [jax.ShapeDtypeStruct((B, T, D), F32)] + [
        jax.ShapeDtypeStruct((B * T * KV_GROUPS, LANES), F32)] * 2 + [
        jax.ShapeDtypeStruct((B, N_HEADS, T, HEAD_W), BF16)] * 2 + [
        jax.ShapeDtypeStruct((B, N_HEADS, HEAD_W, HEAD_W), F32)]
    f = pl.pallas_call(
        functools.partial(_layer_a_kernel, NB=NB, TT=TT, L=L, has_state=has_state),
        grid=(B // NB, nt),
        in_specs=in_specs,
        out_specs=out_specs,
        out_shape=out_shape,
        scratch_shapes=[pltpu.VMEM((R, D), BF16), pltpu.VMEM((R, N_HEADS * HEAD_W + MEM_W), BF16),
                        pltpu.VMEM((N_HEADS, L, L), F32)],
        compiler_params=pltpu.CompilerParams(
            dimension_semantics=("arbitrary", "arbitrary"), vmem_limit_bytes=VMEM_LIMIT_BYTES),
        name="layer_a",
    )
    return f(*args)


def _k_view_to_5d(kr, B, T):
    return kr.reshape(B, T, N_HEADS, 2, DQK)


def _v_view_to_4d(vr, B, T):
    return vr.reshape(B, T, 2, N_HEADS, LANES).transpose(0, 1, 3, 2, 4).reshape(B, T, N_HEADS, HEAD_W)


def _diff_lambda(lp_ref, lam_init):
    lp = lp_ref[...]
    a = jnp.sum(lp[0:1] * lp[1:2], axis=-1, keepdims=True)
    b = jnp.sum(lp[2:3] * lp[3:4], axis=-1, keepdims=True)
    return jnp.exp(a) - jnp.exp(b) + lam_init


def _diff_project_q(h1, nb_ref, win_ref, h_scr, q_scr):
    h_scr[...] = (_rms(h1) * nb_ref[...]).astype(BF16)
    for hh in range(N_HEADS):
        q = _dot(h_scr[...], win_ref[:, hh * HEAD_W:(hh + 1) * HEAD_W]) * (DQK ** -0.5 * LOG2E)
        q_scr[hh] = q.astype(BF16)


def _diff_finish(h1, o_of, lam_init, h_scr, win_ref, sub_ref, mk_ref, mv_ref, wout_ref, nf_ref, cat_scr, y_ref):
    TQ, D = h1.shape
    QW = N_HEADS * HEAD_W
    for hh in range(N_HEADS):
        o = _rms(o_of(hh)) * sub_ref[...] * (1.0 - lam_init)
        g = _dot(h_scr[...], win_ref[:, QW + hh * HEAD_W: QW + (hh + 1) * HEAD_W])
        cat_scr[:, hh * HEAD_W:(hh + 1) * HEAD_W] = (o * _silu(g)).astype(BF16)
    qm = _dot(h_scr[...], win_ref[:, 2 * QW: 2 * QW + MEM_W])
    gm = _dot(h_scr[...], win_ref[:, 2 * QW + MEM_W: 2 * QW + 2 * MEM_W])
    _mem_attention(qm, gm, mk_ref, mv_ref, 0, slice(0, TQ), cat_scr, QW)
    NC = 512
    for c in range(D // NC):
        cs = slice(c * NC, (c + 1) * NC)
        y_ref[0, :, cs] = _dot(cat_scr[...], wout_ref[:, cs]) + h1[:, cs]
    y_ref[0] = _rms(y_ref[0]) * nf_ref[...]


def _layer_b_prompt_kernel(h1_ref, kb_ref, vb_ref, mk_ref, mv_ref, nb_ref, win_ref, lp_ref, sub_ref, wout_ref,
                           nf_ref, y_ref, h_scr, q_scr, o_scr, cat_scr, *, TQ, SUB, NQ, lam_init):
    qi = pl.program_id(0)
    h1 = h1_ref[0]
    _diff_project_q(h1, nb_ref, win_ref, h_scr, q_scr)
    lam = _diff_lambda(lp_ref, lam_init)
    diag = (lax.broadcasted_iota(jnp.int32, (SUB, SUB), 1) // CHUNK) <= (
        lax.broadcasted_iota(jnp.int32, (SUB, SUB), 0) // CHUNK)

    for case in range(NQ):
        @pl.when(qi == case)
        def _(case=case):
            for sub in range(TQ // SUB):
                rows = slice(sub * SUB, (sub + 1) * SUB)
                kt = case * TQ + (sub + 1) * SUB
                for hh in range(N_HEADS):
                    ps, ls = [], []
                    for c in range(2):
                        s = _dot_nt(q_scr[hh, rows, c * DQK:(c + 1) * DQK],
                                    kb_ref[0, hh, :kt, c * DQK:(c + 1) * DQK])
                        last = jnp.where(diag, s[:, kt - SUB:], NEG_BIG)
                        s = last if kt == SUB else jnp.concatenate([s[:, :kt - SUB], last], axis=1)
                        p = jnp.exp2(s - jnp.max(s, axis=-1, keepdims=True))
                        ls.append(jnp.sum(p, axis=-1, keepdims=True))
                        ps.append(p.astype(BF16))
                    v = vb_ref[0, hh, :kt, :]
                    o_scr[hh, rows] = _dot(ps[0], v) / ls[0] - lam * (_dot(ps[1], v) / ls[1])

    _diff_finish(h1, lambda hh: o_scr[hh], lam_init, h_scr, win_ref, sub_ref, mk_ref, mv_ref, wout_ref, nf_ref,
                 cat_scr, y_ref)


def _layer_b_prompt(h1, kb, vb, mkb, mvb, norm_b, w_in, lam_p, subln, w_out, norm_f, *, TQ, SUB, lam_init,
                    mem_layer):
    B, T, D = h1.shape
    kv_spec = pl.BlockSpec((1, N_HEADS, T, HEAD_W), lambda q, b: (b, 0, 0, 0))
    mem_spec = lambda arr: pl.BlockSpec((None, 1) + arr.shape[2:], lambda q, b: (mem_layer, b, 0, 0))
    f = pl.pallas_call(
        functools.partial(_layer_b_prompt_kernel, TQ=TQ, SUB=SUB, NQ=T // TQ, lam_init=lam_init),
        grid=(T // TQ, B),
        in_specs=[pl.BlockSpec((1, TQ, D), lambda q, b: (b, q, 0)), kv_spec, kv_spec,
                  mem_spec(mkb), mem_spec(mvb),
                  _const_spec((1, D)), _const_spec(w_in.shape), _const_spec(lam_p.shape),
                  _const_spec((1, HEAD_W)), _const_spec(w_out.shape), _const_spec((1, D))],
        out_specs=pl.BlockSpec((1, TQ, D), lambda q, b: (b, q, 0)),
        out_shape=jax.ShapeDtypeStruct((B, T, D), F32),
        scratch_shapes=[pltpu.VMEM((TQ, D), BF16), pltpu.VMEM((N_HEADS, TQ, HEAD_W), BF16),
                        pltpu.VMEM((N_HEADS, TQ, HEAD_W), F32),
                        pltpu.VMEM((TQ, N_HEADS * HEAD_W + MEM_W), BF16)],
        compiler_params=pltpu.CompilerParams(
            dimension_semantics=("arbitrary", "arbitrary"), vmem_limit_bytes=VMEM_LIMIT_BYTES),
        name="layer_b_prompt",
    )
    return f(h1, kb, vb, mkb, mvb, norm_b.reshape(1, D), w_in, lam_p, subln.reshape(1, HEAD_W), w_out,
             norm_f.reshape(1, D))


def _online_tile(q_scr, k_of, v_of, m_scr, l_scr, acc_scr):
    TQ = q_scr.shape[1]
    for hh in range(N_HEADS):
        ps, alphas = [], []
        for c in range(2):
            i = 2 * hh + c
            s = _dot_nt(q_scr[hh, :, c * DQK:(c + 1) * DQK], k_of(hh, c))
            m_prev = m_scr[i]
            m_new = jnp.maximum(m_prev, jnp.max(s, axis=-1, keepdims=True))
            alpha = jnp.exp2(m_prev - m_new)
            p = jnp.exp2(s - m_new)
            l_scr[i] = alpha * l_scr[i] + jnp.sum(p, axis=-1, keepdims=True)
            m_scr[i] = m_new
            ps.append(p.astype(BF16))
            alphas.append(alpha)
        pv = _dot(jnp.concatenate(ps, axis=0), v_of(hh))
        acc_scr[2 * hh] = alphas[0] * acc_scr[2 * hh] + pv[:TQ]
        acc_scr[2 * hh + 1] = alphas[1] * acc_scr[2 * hh + 1] + pv[TQ:]


def _layer_b_sample_kernel(h1_ref, ck_ref, cv_ref, kn_ref, vn_ref, mk_ref, mv_ref, nb_ref, win_ref, lp_ref, sub_ref,
                           wout_ref, nf_ref, y_ref, h_scr, q_scr, m_scr, l_scr, acc_scr, cat_scr, *, TK, lam_init):
    kt = pl.program_id(1)

    @pl.when(kt == 0)
    def _():
        _diff_project_q(h1_ref[0], nb_ref, win_ref, h_scr, q_scr)
        m_scr[...] = jnp.full(m_scr.shape, NEG_BIG, F32)
        l_scr[...] = jnp.zeros(l_scr.shape, F32)
        acc_scr[...] = jnp.zeros(acc_scr.shape, F32)

    def ck_of(hh, c):
        return ck_ref[pl.ds(2 * hh + c, TK, stride=KV_GROUPS), :].astype(BF16)

    def cv_of(hh):
        return jnp.concatenate([cv_ref[pl.ds(_v_row(hh, dt), TK, stride=KV_GROUPS), :] for dt in range(2)],
                               axis=1).astype(BF16)

    _online_tile(q_scr, ck_of, cv_of, m_scr, l_scr, acc_scr)

    @pl.when(kt == pl.num_programs(1) - 1)
    def _():
        _online_tile(q_scr,
                     lambda hh, c: kn_ref[0, hh, :, c * DQK:(c + 1) * DQK],
                     lambda hh: vn_ref[0, hh],
                     m_scr, l_scr, acc_scr)
        lam = _diff_lambda(lp_ref, lam_init)
        o_of = lambda hh: (acc_scr[2 * hh] / l_scr[2 * hh]
                           - lam * (acc_scr[2 * hh + 1] / l_scr[2 * hh + 1]))
        _diff_finish(h1_ref[0], o_of, lam_init, h_scr, win_ref, sub_ref, mk_ref, mv_ref, wout_ref, nf_ref,
                     cat_scr, y_ref)


def _layer_b_sample(h1, ck_rows, cv_rows, kb, vb, mkb, mvb, norm_b, w_in, lam_p, subln, w_out, norm_f, *, P, TK,
                    lam_init, mem_layer):
    B, T, D = h1.shape
    nkt = P // TK
    b0 = lambda b, k: (b, 0, 0)
    cache_spec = pl.BlockSpec((TK * KV_GROUPS, LANES), lambda b, k: (b * nkt + k, 0))
    new_spec = pl.BlockSpec((1, N_HEADS, T, HEAD_W), lambda b, k: (b, 0, 0, 0))
    f = pl.pallas_call(
        functools.partial(_layer_b_sample_kernel, TK=TK, lam_init=lam_init),
        grid=(B, nkt),
        in_specs=[pl.BlockSpec((1, T, D), b0), cache_spec, cache_spec, new_spec, new_spec,
                  _mem_spec(mkb, mem_layer, 1), _mem_spec(mvb, mem_layer, 1),
                  _const_spec((1, D)), _const_spec(w_in.shape), _const_spec(lam_p.shape),
                  _const_spec((1, HEAD_W)), _const_spec(w_out.shape), _const_spec((1, D))],
        out_specs=pl.BlockSpec((1, T, D), b0),
        out_shape=jax.ShapeDtypeStruct((B, T, D), F32),
        scratch_shapes=[pltpu.VMEM((T, D), BF16), pltpu.VMEM((N_HEADS, T, HEAD_W), BF16),
                        pltpu.VMEM((2 * N_HEADS, T, 1), F32), pltpu.VMEM((2 * N_HEADS, T, 1), F32),
                        pltpu.VMEM((2 * N_HEADS, T, HEAD_W), F32),
                        pltpu.VMEM((T, N_HEADS * HEAD_W + MEM_W), BF16)],
        compiler_params=pltpu.CompilerParams(
            dimension_semantics=("arbitrary", "arbitrary"), vmem_limit_bytes=VMEM_LIMIT_BYTES),
        name="layer_b_sample",
    )
    return f(h1, ck_rows, cv_rows, kb, vb, mkb, mvb, norm_b.reshape(1, D), w_in, lam_p, subln.reshape(1, HEAD_W),
             w_out, norm_f.reshape(1, D))


def _rope_tables(pos):
    half = DQK
    inv = ROPE_BASE ** (-jnp.arange(half, dtype=F32) / half)
    ang = pos.astype(F32)[:, None] * inv[None, :]
    return jnp.cos(ang), jnp.sin(ang)


def _pick(n, candidates):
    for c in candidates:
        if n % c == 0:
            return c
    return n


def kernel(x_prompt, x_sample, state_ret, cache_k, cache_v, cache_mem_k, cache_mem_v, mem_prompt,
           norm_a, w_in_a, w_out_a, norm_kv, w_kv, norm_b, w_in_b, diff_lambda, subln_b, w_out_b,
           norm_mem, w_mem_kv, norm_f):
    depth = norm_mem.shape[0]
    assert norm_a.shape[0] == 1 and norm_b.shape[0] == 1 and depth == 2, "kernel is written for depth 2"
    lam_init = 0.8 - 0.6 * math.exp(-0.3 * 1)
    wia, woa, wkv = w_in_a[0].astype(BF16), w_out_a[0].astype(BF16), w_kv.astype(BF16)
    wib, wob = w_in_b[0].astype(BF16), w_out_b[0].astype(BF16)

    Bp, Tp, D = x_prompt.shape
    mk_p, mv_p, mkb_p, mvb_p = _memory_kv(mem_prompt, norm_mem, w_mem_kv)
    cos_p, sin_p = _rope_tables(jnp.arange(Tp))
    blk = _pick(Tp, (256, 128, CHUNK))
    tt = _pick(Tp, (2 * blk, blk))
    h1_p, k_p, v_p, kb_p, vb_p, S_p = _layer_a(
        x_prompt, cos_p, sin_p, None, mkb_p, mvb_p, norm_a[0], wia, woa, norm_kv, wkv, NB=1, TT=tt, L=blk,
        mem_layer=0)
    y_p = _layer_b_prompt(h1_p, kb_p, vb_p, mkb_p, mvb_p, norm_b[0], wib, diff_lambda[0], subln_b[0], wob,
                          norm_f, TQ=blk, SUB=blk, lam_init=lam_init, mem_layer=1)

    Bs, Ts, _ = x_sample.shape
    past = cache_k.shape[1]
    nbs = _pick(Bs, (4, 2, 1))
    cos_s, sin_s = _rope_tables(past + jnp.arange(Ts))
    cos_s, sin_s = jnp.tile(cos_s, (nbs, 1)), jnp.tile(sin_s, (nbs, 1))
    M = cache_mem_k.shape[2]
    cmk = cache_mem_k.reshape(depth, Bs, M * N_HEADS, MEM_HD)
    cmv = cache_mem_v.reshape(depth, Bs, M * N_HEADS, MEM_HD)
    h1_s, k_s, v_s, kb_s, vb_s, S_s = _layer_a(
        x_sample, cos_s, sin_s, state_ret[0], cmk, cmv, norm_a[0], wia, woa, norm_kv, wkv,
        NB=nbs, TT=Ts, L=Ts, mem_layer=0)
    ck_rows = cache_k.reshape(Bs * past * KV_GROUPS, LANES)
    cv_rows = cache_v.reshape(Bs, past, N_HEADS, 2, LANES).transpose(0, 1, 3, 2, 4).reshape(
        Bs * past * KV_GROUPS, LANES)
    y_s = _layer_b_sample(h1_s, ck_rows, cv_rows, kb_s, vb_s, cmk, cmv, norm_b[0], wib, diff_lambda[0],
                          subln_b[0], wob, norm_f, P=past, TK=_pick(past, (1024, 512, 256, 128)), lam_init=lam_init,
                          mem_layer=1)

    return (y_p, y_s, S_p[None], _k_view_to_5d(k_p, Bp, Tp), _v_view_to_4d(v_p, Bp, Tp),
            mk_p, mv_p, S_s[None], _k_view_to_5d(k_s, Bs, Ts), _v_view_to_4d(v_s, Bs, Ts))
```

```python
import functools
import math

import jax
import jax.numpy as jnp
from jax import lax
from jax.experimental import pallas as pl
from jax.experimental.pallas import tpu as pltpu

F32 = jnp.float32
BF16 = jnp.bfloat16

EPS = 1e-6
ROPE_BASE = 10000.0
CHUNK = 64
N_HEADS = 4
HEAD_W = 256
DQK = 128
MEM_HD = 128
MEM_W = N_HEADS * MEM_HD
LANES = 128
KV_GROUPS = N_HEADS * HEAD_W // LANES
NEG_BIG = -1e30
LOG2E = math.log2(math.e)

VMEM_LIMIT_BYTES = 56 * 1024 * 1024


def _dot(a, b):
    return jnp.dot(a, b, preferred_element_type=F32)


def _dot_nt(a, b):
    return lax.dot_general(a, b, (((1,), (1,)), ((), ())), preferred_element_type=F32)


def _dot_tn(a, b):
    return lax.dot_general(a, b, (((0,), (0,)), ((), ())), preferred_element_type=F32)


def _rms(x):
    return x * lax.rsqrt(jnp.mean(x * x, axis=-1, keepdims=True) + EPS)


def _silu(g):
    return g / (1.0 + jnp.exp(-g))


def _mem_spec(arr, layer, nb):
    zeros = (0,) * (arr.ndim - 2)
    return pl.BlockSpec((None, nb) + arr.shape[2:], lambda b, t: (layer, b) + zeros)


def _const_spec(shape):
    nd = len(shape)
    return pl.BlockSpec(shape, lambda *_: (0,) * nd, pipeline_mode=pl.Buffered(1))


def _v_row(h, dt):
    return dt * N_HEADS + h


def _mem_attention(qm, gm, mk_ref, mv_ref, nb, rows, cat_scr, col0):
    row_view = len(mk_ref.shape) == 3
    for hh in range(N_HEADS):
        cs = slice(hh * MEM_HD, (hh + 1) * MEM_HD)
        if row_view:
            hsel = pl.ds(hh, mk_ref.shape[1] // N_HEADS, stride=N_HEADS)
            mk, mv = mk_ref[nb, hsel, :].astype(BF16), mv_ref[nb, hsel, :].astype(BF16)
        else:
            mk, mv = mk_ref[nb, hh], mv_ref[nb, hh]
        q = (qm[rows, cs] * (MEM_HD ** -0.5 * LOG2E)).astype(BF16)
        s = _dot_nt(q, mk)
        p = jnp.exp2(s - jnp.max(s, axis=-1, keepdims=True))
        l = jnp.sum(p, axis=-1, keepdims=True)
        o = _dot(p.astype(BF16), mv) / l
        cat_scr[rows, col0 + hh * MEM_HD: col0 + (hh + 1) * MEM_HD] = (o * _silu(gm[rows, cs])).astype(BF16)


def _memkv_kernel(mem_ref, g_ref, w_ref, mk_ref, mv_ref, mkb_ref, mvb_ref):
    tm = mem_ref.shape[0]
    hb = (_rms(mem_ref[...]) * g_ref[0]).astype(BF16)
    kv = _dot(hb, w_ref[0])
    mk, mv = kv[:, :MEM_W], kv[:, MEM_W:]
    nbm, M = mkb_ref.shape[1], mkb_ref.shape[3]
    for hh in range(N_HEADS):
        cs = slice(hh * MEM_HD, (hh + 1) * MEM_HD)
        mk_ref[0, pl.ds(hh, tm, stride=N_HEADS), :] = mk[:, cs]
        mv_ref[0, pl.ds(hh, tm, stride=N_HEADS), :] = mv[:, cs]
        for bb in range(nbm):
            mkb_ref[0, bb, hh] = mk[bb * M:(bb + 1) * M, cs].astype(BF16)
            mvb_ref[0, bb, hh] = mv[bb * M:(bb + 1) * M, cs].astype(BF16)


def _memory_kv(mem, norm_mem, w_mem_kv):
    B, M, D = mem.shape
    nl = norm_mem.shape[0]
    rows = B * M
    tm = 512 if rows % 512 == 0 else M
    f = pl.pallas_call(
        _memkv_kernel,
        grid=(nl, rows // tm),
        in_specs=[
            pl.BlockSpec((tm, D), lambda l, i: (i, 0)),
            pl.BlockSpec((1, 1, D), lambda l, i: (l, 0, 0)),
            pl.BlockSpec((1, D, 2 * MEM_W), lambda l, i: (l, 0, 0)),
        ],
        out_specs=[pl.BlockSpec((1, tm * N_HEADS, MEM_HD), lambda l, i: (l, i, 0))] * 2
        + [pl.BlockSpec((1, tm // M, N_HEADS, M, MEM_HD), lambda l, i: (l, i, 0, 0, 0))] * 2,
        out_shape=[jax.ShapeDtypeStruct((nl, rows * N_HEADS, MEM_HD), F32)] * 2
        + [jax.ShapeDtypeStruct((nl, B, N_HEADS, M, MEM_HD), BF16)] * 2,
        compiler_params=pltpu.CompilerParams(
            dimension_semantics=("arbitrary", "arbitrary"), vmem_limit_bytes=VMEM_LIMIT_BYTES),
        name="memkv",
    )
    mk, mv, mkb, mvb = f(mem.reshape(rows, D), norm_mem.reshape(nl, 1, D), w_mem_kv.astype(BF16))
    shp5 = (nl, B, M, N_HEADS, MEM_HD)
    return mk.reshape(shp5), mv.reshape(shp5), mkb, mvb


def _layer_a_kernel(*refs, NB, TT, L, has_state):
    it = iter(refs)
    x_ref, cos_ref, sin_ref = next(it), next(it), next(it)
    s0_ref = next(it) if has_state else None
    mk_ref, mv_ref, na_ref, win_ref, wout_ref, nkv_ref, wkv_ref = (next(it) for _ in range(7))
    h1_ref, k_ref, v_ref, kb_ref, vb_ref, S_ref = (next(it) for _ in range(6))
    h_scr, cat_scr, dec_scr = (next(it) for _ in range(3))

    R = NB * TT
    D = x_ref.shape[-1]
    QW = N_HEADS * HEAD_W
    t = pl.program_id(1)
    log_g = [math.log(1.0 - 2.0 ** (-5.0 - h)) for h in range(N_HEADS)]

    @pl.when((pl.program_id(0) == 0) & (t == 0))
    def _():
        d = (lax.broadcasted_iota(jnp.int32, (L, L), 0) - lax.broadcasted_iota(jnp.int32, (L, L), 1)).astype(F32)
        for h in range(N_HEADS):
            dec_scr[h] = jnp.where(d >= 0, jnp.exp(log_g[h] * jnp.maximum(d, 0.0)), 0.0)

    @pl.when(t == 0)
    def _():
        if has_state:
            S_ref[...] = s0_ref[...]
        else:
            S_ref[...] = jnp.zeros(S_ref.shape, F32)

    x = x_ref[...].reshape(R, D)
    h_scr[...] = (_rms(x) * na_ref[...]).astype(BF16)
    cos = cos_ref[...]
    sin = sin_ref[...]
    half = HEAD_W // 2
    idx = lax.broadcasted_iota(jnp.int32, (L, 1), 0).astype(F32)

    def rope(c0):
        y = _dot(h_scr[...], win_ref[:, c0:c0 + HEAD_W])
        y1, y2 = y[:, :half], y[:, half:]
        return jnp.concatenate([y1 * cos - y2 * sin, y2 * cos + y1 * sin], axis=-1)

    for h in range(N_HEADS):
        q = rope(h * HEAD_W).astype(BF16)
        k = rope(QW + h * HEAD_W) * (HEAD_W ** -0.5)
        v = _dot(h_scr[...], win_ref[:, 2 * QW + h * HEAD_W: 2 * QW + (h + 1) * HEAD_W]).astype(BF16)
        g = _dot(h_scr[...], win_ref[:, 3 * QW + h * HEAD_W: 3 * QW + (h + 1) * HEAD_W])
        a_in = jnp.exp(log_g[h] * (idx + 1.0))
        b_out = jnp.exp(log_g[h] * (L - 1.0 - idx))
        g_blk = math.exp(log_g[h] * L)
        for nb in range(NB):
            for j in range(TT // L):
                rows = slice(nb * TT + j * L, nb * TT + (j + 1) * L)
                qb, vb = q[rows], v[rows]
                kf = k[rows]
                S = S_ref[nb, h]
                sc = _dot_nt(qb, kf.astype(BF16)) * dec_scr[h]
                o = _dot(sc.astype(BF16), vb) + _dot(qb, S.astype(BF16)) * a_in
                S_ref[nb, h] = g_blk * S + _dot_tn((kf * b_out).astype(BF16), vb)
                cat_scr[rows, h * HEAD_W:(h + 1) * HEAD_W] = (_rms(o) * _silu(g[rows])).astype(BF16)

    qm = _dot(h_scr[...], win_ref[:, 4 * QW: 4 * QW + MEM_W])
    gm = _dot(h_scr[...], win_ref[:, 4 * QW + MEM_W: 4 * QW + 2 * MEM_W])
    for nb in range(NB):
        _mem_attention(qm, gm, mk_ref, mv_ref, nb, slice(nb * TT, (nb + 1) * TT), cat_scr, QW)

    NC = 512
    for c in range(D // NC):
        cs = slice(c * NC, (c + 1) * NC)
        y = _dot(cat_scr[...], wout_ref[:, cs]) + x_ref[:, :, cs].reshape(R, NC)
        h1_ref[:, :, cs] = y.reshape(NB, TT, NC)

    hk = (_rms(h1_ref[...].reshape(R, D)) * nkv_ref[...]).astype(BF16)
    KW = N_HEADS * HEAD_W
    for c in range(2 * KW // NC):
        kv = _dot(hk, wkv_ref[:, c * NC:(c + 1) * NC])
        is_k = c < KW // NC
        col0 = c * NC if is_k else c * NC - KW
        for hl in range(NC // HEAD_W):
            (kb_ref if is_k else vb_ref)[:, col0 // HEAD_W + hl] = (
                kv[:, hl * HEAD_W:(hl + 1) * HEAD_W].astype(BF16).reshape(NB, TT, HEAD_W))
        for jj in range(NC // LANES):
            col = col0 + jj * LANES
            if is_k:
                row = col // LANES
                k_ref[pl.ds(row, R, stride=KV_GROUPS), :] = kv[:, jj * LANES:(jj + 1) * LANES]
            else:
                row = _v_row(col // HEAD_W, (col % HEAD_W) // LANES)
                v_ref[pl.ds(row, R, stride=KV_GROUPS), :] = kv[:, jj * LANES:(jj + 1) * LANES]


def _layer_a(x, cos, sin, s0, mkb, mvb, norm_a, w_in, w_out, norm_kv, w_kv, *, NB, TT, L, mem_layer):
    B, T, D = x.shape
    R = NB * TT
    nt = T // TT
    assert NB == 1 or nt == 1
    KW = w_kv.shape[1] // 2
    has_state = s0 is not None
    bt = lambda b, t: (b, t, 0)
    b0 = lambda b, t: (b, 0, 0)
    in_specs = [pl.BlockSpec((NB, TT, D), bt),
                pl.BlockSpec((R, DQK), lambda b, t: (t, 0)),
                pl.BlockSpec((R, DQK), lambda b, t: (t, 0))]
    args = [x, cos, sin]
    if has_state:
        in_specs.append(pl.BlockSpec((NB, N_HEADS, HEAD_W, HEAD_W), lambda b, t: (b, 0, 0, 0)))
        args.append(s0)
    in_specs += [_mem_spec(mkb, mem_layer, NB), _mem_spec(mvb, mem_layer, NB),
                 _const_spec((1, D)), _const_spec(w_in.shape), _const_spec(w_out.shape),
                 _const_spec((1, D)), _const_spec(w_kv.shape)]
    args += [mkb, mvb, norm_a.reshape(1, D), w_in, w_out, norm_kv.reshape(1, D), w_kv]
    rows_spec = pl.BlockSpec((R * KV_GROUPS, LANES), lambda b, t: (b * nt + t, 0))
    assert KW == N_HEADS * HEAD_W
    hm_spec = pl.BlockSpec((NB, N_HEADS, TT, HEAD_W), lambda b, t: (b, 0, t, 0))
    out_specs = [pl.BlockSpec((NB, TT, D), bt), rows_spec, rows_spec, hm_spec, hm_spec,
                 pl.BlockSpec((NB, N_HEADS, HEAD_W, HEAD_W), lambda b, t: (b, 0, 0, 0))]
    out_shape = [jax.ShapeDtypeStruct((B, T, D), F32)] + [
        jax.ShapeDtypeStruct((B * T * KV_GROUPS, LANES), F32)] * 2 + [
        jax.ShapeDtypeStruct((B, N_HEADS, T, HEAD_W), BF16)] * 2 + [
        jax.ShapeDtypeStruct((B, N_HEADS, HEAD_W, HEAD_W), F32)]
    f = pl.pallas_call(
        functools.partial(_layer_a_kernel, NB=NB, TT=TT, L=L, has_state=has_state),
        grid=(B // NB, nt),
        in_specs=in_specs,
        out_specs=out_specs,
        out_shape=out_shape,
        scratch_shapes=[pltpu.VMEM((R, D), BF16), pltpu.VMEM((R, N_HEADS * HEAD_W + MEM_W), BF16),
                        pltpu.VMEM((N_HEADS, L, L), F32)],
        compiler_params=pltpu.CompilerParams(
            dimension_semantics=("arbitrary", "arbitrary"), vmem_limit_bytes=VMEM_LIMIT_BYTES),
        name="layer_a",
    )
    return f(*args)


def _k_view_to_5d(kr, B, T):
    return kr.reshape(B, T, N_HEADS, 2, DQK)


def _v_view_to_4d(vr, B, T):
    return vr.reshape(B, T, 2, N_HEADS, LANES).transpose(0, 1, 3, 2, 4).reshape(B, T, N_HEADS, HEAD_W)


def _diff_lambda(lp_ref, lam_init):
    lp = lp_ref[...]
    a = jnp.sum(lp[0:1] * lp[1:2], axis=-1, keepdims=True)
    b = jnp.sum(lp[2:3] * lp[3:4], axis=-1, keepdims=True)
    return jnp.exp(a) - jnp.exp(b) + lam_init


def _diff_project_q(h1, nb_ref, win_ref, h_scr, q_scr):
    h_scr[...] = (_rms(h1) * nb_ref[...]).astype(BF16)
    for hh in range(N_HEADS):
        q = _dot(h_scr[...], win_ref[:, hh * HEAD_W:(hh + 1) * HEAD_W]) * (DQK ** -0.5 * LOG2E)
        q_scr[hh] = q.astype(BF16)


def _diff_finish(h1, o_of, lam_init, h_scr, win_ref, sub_ref, mk_ref, mv_ref, wout_ref, nf_ref, cat_scr, y_ref):
    TQ, D = h1.shape
    QW = N_HEADS * HEAD_W
    for hh in range(N_HEADS):
        o = _rms(o_of(hh)) * sub_ref[...] * (1.0 - lam_init)
        g = _dot(h_scr[...], win_ref[:, QW + hh * HEAD_W: QW + (hh + 1) * HEAD_W])
        cat_scr[:, hh * HEAD_W:(hh + 1) * HEAD_W] = (o * _silu(g)).astype(BF16)
    qm = _dot(h_scr[...], win_ref[:, 2 * QW: 2 * QW + MEM_W])
    gm = _dot(h_scr[...], win_ref[:, 2 * QW + MEM_W: 2 * QW + 2 * MEM_W])
    _mem_attention(qm, gm, mk_ref, mv_ref, 0, slice(0, TQ), cat_scr, QW)
    NC = 512
    for c in range(D // NC):
        cs = slice(c * NC, (c + 1) * NC)
        y_ref[0, :, cs] = _dot(cat_scr[...], wout_ref[:, cs]) + h1[:, cs]
    y_ref[0] = _rms(y_ref[0]) * nf_ref[...]


def _layer_b_prompt_kernel(h1_ref, kb_ref, vb_ref, mk_ref, mv_ref, nb_ref, win_ref, lp_ref, sub_ref, wout_ref,
                           nf_ref, y_ref, h_scr, q_scr, o_scr, cat_scr, *, TQ, SUB, NQ, lam_init):
    qi = pl.program_id(1)
    h1 = h1_ref[0]
    _diff_project_q(h1, nb_ref, win_ref, h_scr, q_scr)
    lam = _diff_lambda(lp_ref, lam_init)
    diag = (lax.broadcasted_iota(jnp.int32, (SUB, SUB), 1) // CHUNK) <= (
        lax.broadcasted_iota(jnp.int32, (SUB, SUB), 0) // CHUNK)

    for case in range(NQ):
        @pl.when(qi == case)
        def _(case=case):
            for sub in range(TQ // SUB):
                rows = slice(sub * SUB, (sub + 1) * SUB)
                kt = case * TQ + (sub + 1) * SUB
                for hh in range(N_HEADS):
                    ps, ls = [], []
                    for c in range(2):
                        s = _dot_nt(q_scr[hh, rows, c * DQK:(c + 1) * DQK],
                                    kb_ref[0, hh, :kt, c * DQK:(c + 1) * DQK])
                        last = jnp.where(diag, s[:, kt - SUB:], NEG_BIG)
                        s = last if kt == SUB else jnp.concatenate([s[:, :kt - SUB], last], axis=1)
                        p = jnp.exp2(s - jnp.max(s, axis=-1, keepdims=True))
                        ls.append(jnp.sum(p, axis=-1, keepdims=True))
                        ps.append(p.astype(BF16))
                    v = vb_ref[0, hh, :kt, :]
                    o_scr[hh, rows] = _dot(ps[0], v) / ls[0] - lam * (_dot(ps[1], v) / ls[1])

    _diff_finish(h1, lambda hh: o_scr[hh], lam_init, h_scr, win_ref, sub_ref, mk_ref, mv_ref, wout_ref, nf_ref,
                 cat_scr, y_ref)


def _layer_b_prompt(h1, kb, vb, mkb, mvb, norm_b, w_in, lam_p, subln, w_out, norm_f, *, TQ, SUB, lam_init,
                    mem_layer):
    B, T, D = h1.shape
    kv_spec = pl.BlockSpec((1, N_HEADS, T, HEAD_W), lambda b, q: (b, 0, 0, 0))
    f = pl.pallas_call(
        functools.partial(_layer_b_prompt_kernel, TQ=TQ, SUB=SUB, NQ=T // TQ, lam_init=lam_init),
        grid=(B, T // TQ),
        in_specs=[pl.BlockSpec((1, TQ, D), lambda b, q: (b, q, 0)), kv_spec, kv_spec,
                  _mem_spec(mkb, mem_layer, 1), _mem_spec(mvb, mem_layer, 1),
                  _const_spec((1, D)), _const_spec(w_in.shape), _const_spec(lam_p.shape),
                  _const_spec((1, HEAD_W)), _const_spec(w_out.shape), _const_spec((1, D))],
        out_specs=pl.BlockSpec((1, TQ, D), lambda b, q: (b, q, 0)),
        out_shape=jax.ShapeDtypeStruct((B, T, D), F32),
        scratch_shapes=[pltpu.VMEM((TQ, D), BF16), pltpu.VMEM((N_HEADS, TQ, HEAD_W), BF16),
                        pltpu.VMEM((N_HEADS, TQ, HEAD_W), F32),
                        pltpu.VMEM((TQ, N_HEADS * HEAD_W + MEM_W), BF16)],
        compiler_params=pltpu.CompilerParams(
            dimension_semantics=("arbitrary", "arbitrary"), vmem_limit_bytes=VMEM_LIMIT_BYTES),
        name="layer_b_prompt",
    )
    return f(h1, kb, vb, mkb, mvb, norm_b.reshape(1, D), w_in, lam_p, subln.reshape(1, HEAD_W), w_out,
             norm_f.reshape(1, D))


def _online_tile(q_scr, k_of, v_of, m_scr, l_scr, acc_scr):
    TQ = q_scr.shape[1]
    for hh in range(N_HEADS):
        ps, alphas = [], []
        for c in range(2):
            i = 2 * hh + c
            s = _dot_nt(q_scr[hh, :, c * DQK:(c + 1) * DQK], k_of(hh, c))
            m_prev = m_scr[i]
            m_new = jnp.maximum(m_prev, jnp.max(s, axis=-1, keepdims=True))
            alpha = jnp.exp2(m_prev - m_new)
            p = jnp.exp2(s - m_new)
            l_scr[i] = alpha * l_scr[i] + jnp.sum(p, axis=-1, keepdims=True)
            m_scr[i] = m_new
            ps.append(p.astype(BF16))
            alphas.append(alpha)
        pv = _dot(jnp.concatenate(ps, axis=0), v_of(hh))
        acc_scr[2 * hh] = alphas[0] * acc_scr[2 * hh] + pv[:TQ]
        acc_scr[2 * hh + 1] = alphas[1] * acc_scr[2 * hh + 1] + pv[TQ:]


def _layer_b_sample_kernel(h1_ref, ck_ref, cv_ref, kn_ref, vn_ref, mk_ref, mv_ref, nb_ref, win_ref, lp_ref, sub_ref,
                           wout_ref, nf_ref, y_ref, h_scr, q_scr, m_scr, l_scr, acc_scr, cat_scr, *, TK, lam_init):
    kt = pl.program_id(1)

    @pl.when(kt == 0)
    def _():
        _diff_project_q(h1_ref[0], nb_ref, win_ref, h_scr, q_scr)
        m_scr[...] = jnp.full(m_scr.shape, NEG_BIG, F32)
        l_scr[...] = jnp.zeros(l_scr.shape, F32)
        acc_scr[...] = jnp.zeros(acc_scr.shape, F32)

    def ck_of(hh, c):
        return ck_ref[pl.ds(2 * hh + c, TK, stride=KV_GROUPS), :].astype(BF16)

    def cv_of(hh):
        return jnp.concatenate([cv_ref[pl.ds(_v_row(hh, dt), TK, stride=KV_GROUPS), :] for dt in range(2)],
                               axis=1).astype(BF16)

    _online_tile(q_scr, ck_of, cv_of, m_scr, l_scr, acc_scr)

    @pl.when(kt == pl.num_programs(1) - 1)
    def _():
        _online_tile(q_scr,
                     lambda hh, c: kn_ref[0, hh, :, c * DQK:(c + 1) * DQK],
                     lambda hh: vn_ref[0, hh],
                     m_scr, l_scr, acc_scr)
        lam = _diff_lambda(lp_ref, lam_init)
        o_of = lambda hh: (acc_scr[2 * hh] / l_scr[2 * hh]
                           - lam * (acc_scr[2 * hh + 1] / l_scr[2 * hh + 1]))
        _diff_finish(h1_ref[0], o_of, lam_init, h_scr, win_ref, sub_ref, mk_ref, mv_ref, wout_ref, nf_ref,
                     cat_scr, y_ref)


def _layer_b_sample(h1, ck_rows, cv_rows, kb, vb, mkb, mvb, norm_b, w_in, lam_p, subln, w_out, norm_f, *, P, TK,
                    lam_init, mem_layer):
    B, T, D = h1.shape
    nkt = P // TK
    b0 = lambda b, k: (b, 0, 0)
    cache_spec = pl.BlockSpec((TK * KV_GROUPS, LANES), lambda b, k: (b * nkt + k, 0))
    new_spec = pl.BlockSpec((1, N_HEADS, T, HEAD_W), lambda b, k: (b, 0, 0, 0))
    f = pl.pallas_call(
        functools.partial(_layer_b_sample_kernel, TK=TK, lam_init=lam_init),
        grid=(B, nkt),
        in_specs=[pl.BlockSpec((1, T, D), b0), cache_spec, cache_spec, new_spec, new_spec,
                  _mem_spec(mkb, mem_layer, 1), _mem_spec(mvb, mem_layer, 1),
                  _const_spec((1, D)), _const_spec(w_in.shape), _const_spec(lam_p.shape),
                  _const_spec((1, HEAD_W)), _const_spec(w_out.shape), _const_spec((1, D))],
        out_specs=pl.BlockSpec((1, T, D), b0),
        out_shape=jax.ShapeDtypeStruct((B, T, D), F32),
        scratch_shapes=[pltpu.VMEM((T, D), BF16), pltpu.VMEM((N_HEADS, T, HEAD_W), BF16),
                        pltpu.VMEM((2 * N_HEADS, T, 1), F32), pltpu.VMEM((2 * N_HEADS, T, 1), F32),
                        pltpu.VMEM((2 * N_HEADS, T, HEAD_W), F32),
                        pltpu.VMEM((T, N_HEADS * HEAD_W + MEM_W), BF16)],
        compiler_params=pltpu.CompilerParams(
            dimension_semantics=("arbitrary", "arbitrary"), vmem_limit_bytes=VMEM_LIMIT_BYTES),
        name="layer_b_sample",
    )
    return f(h1, ck_rows, cv_rows, kb, vb, mkb, mvb, norm_b.reshape(1, D), w_in, lam_p, subln.reshape(1, HEAD_W),
             w_out, norm_f.reshape(1, D))


def _rope_tables(pos):
    half = DQK
    inv = ROPE_BASE ** (-jnp.arange(half, dtype=F32) / half)
    ang = pos.astype(F32)[:, None] * inv[None, :]
    return jnp.cos(ang), jnp.sin(ang)


def _pick(n, candidates):
    for c in candidates:
        if n % c == 0:
            return c
    return n


def kernel(x_prompt, x_sample, state_ret, cache_k, cache_v, cache_mem_k, cache_mem_v, mem_prompt,
           norm_a, w_in_a, w_out_a, norm_kv, w_kv, norm_b, w_in_b, diff_lambda, subln_b, w_out_b,
           norm_mem, w_mem_kv, norm_f):
    depth = norm_mem.shape[0]
    assert norm_a.shape[0] == 1 and norm_b.shape[0] == 1 and depth == 2, "kernel is written for depth 2"
    lam_init = 0.8 - 0.6 * math.exp(-0.3 * 1)
    wia, woa, wkv = w_in_a[0].astype(BF16), w_out_a[0].astype(BF16), w_kv.astype(BF16)
    wib, wob = w_in_b[0].astype(BF16), w_out_b[0].astype(BF16)

    Bp, Tp, D = x_prompt.shape
    mk_p, mv_p, mkb_p, mvb_p = _memory_kv(mem_prompt, norm_mem, w_mem_kv)
    cos_p, sin_p = _rope_tables(jnp.arange(Tp))
    blk = _pick(Tp, (256, 128, CHUNK))
    tt = _pick(Tp, (2 * blk, blk))
    h1_p, k_p, v_p, kb_p, vb_p, S_p = _layer_a(
        x_prompt, cos_p, sin_p, None, mkb_p, mvb_p, norm_a[0], wia, woa, norm_kv, wkv, NB=1, TT=tt, L=blk,
        mem_layer=0)
    y_p = _layer_b_prompt(h1_p, kb_p, vb_p, mkb_p, mvb_p, norm_b[0], wib, diff_lambda[0], subln_b[0], wob,
                          norm_f, TQ=blk, SUB=blk, lam_init=lam_init, mem_layer=1)

    Bs, Ts, _ = x_sample.shape
    past = cache_k.shape[1]
    nbs = _pick(Bs, (4, 2, 1))
    cos_s, sin_s = _rope_tables(past + jnp.arange(Ts))
    cos_s, sin_s = jnp.tile(cos_s, (nbs, 1)), jnp.tile(sin_s, (nbs, 1))
    M = cache_mem_k.shape[2]
    cmk = cache_mem_k.reshape(depth, Bs, M * N_HEADS, MEM_HD)
    cmv = cache_mem_v.reshape(depth, Bs, M * N_HEADS, MEM_HD)
    h1_s, k_s, v_s, kb_s, vb_s, S_s = _layer_a(
        x_sample, cos_s, sin_s, state_ret[0], cmk, cmv, norm_a[0], wia, woa, norm_kv, wkv,
        NB=nbs, TT=Ts, L=Ts, mem_layer=0)
    ck_rows = cache_k.reshape(Bs * past * KV_GROUPS, LANES)
    cv_rows = cache_v.reshape(Bs, past, N_HEADS, 2, LANES).transpose(0, 1, 3, 2, 4).reshape(
        Bs * past * KV_GROUPS, LANES)
    y_s = _layer_b_sample(h1_s, ck_rows, cv_rows, kb_s, vb_s, cmk, cmv, norm_b[0], wib, diff_lambda[0],
                          subln_b[0], wob, norm_f, P=past, TK=_pick(past, (2048, 1024, 512, 256, 128)), lam_init=lam_init,
                          mem_layer=1)

    return (y_p, y_s, S_p[None], _k_view_to_5d(k_p, Bp, Tp), _v_view_to_4d(v_p, Bp, Tp),
            mk_p, mv_p, S_s[None], _k_view_to_5d(k_s, Bs, Ts), _v_view_to_4d(v_s, Bs, Ts))
```

```python
import functools
import math

import jax
import jax.numpy as jnp
from jax import lax
from jax.experimental import pallas as pl
from jax.experimental.pallas import tpu as pltpu

F32 = jnp.float32
BF16 = jnp.bfloat16

EPS = 1e-6
ROPE_BASE = 10000.0
CHUNK = 64
N_HEADS = 4
HEAD_W = 256
DQK = 128
MEM_HD = 128
MEM_W = N_HEADS * MEM_HD
LANES = 128
KV_GROUPS = N_HEADS * HEAD_W // LANES
NEG_BIG = -1e30
LOG2E = math.log2(math.e)

VMEM_LIMIT_BYTES = 56 * 1024 * 1024


def _dot(a, b):
    return jnp.dot(a, b, preferred_element_type=F32)


def _dot_nt(a, b):
    return lax.dot_general(a, b, (((1,), (1,)), ((), ())), preferred_element_type=F32)


def _dot_tn(a, b):
    return lax.dot_general(a, b, (((0,), (0,)), ((), ())), preferred_element_type=F32)


def _rms(x):
    return x * lax.rsqrt(jnp.mean(x * x, axis=-1, keepdims=True) + EPS)


def _silu(g):
    return (0.5 * g) * (1.0 + jnp.tanh(0.5 * g))


def _mem_spec(arr, layer, nb):
    zeros = (0,) * (arr.ndim - 2)
    return pl.BlockSpec((None, nb) + arr.shape[2:], lambda b, t: (layer, b) + zeros)


def _const_spec(shape):
    nd = len(shape)
    return pl.BlockSpec(shape, lambda *_: (0,) * nd, pipeline_mode=pl.Buffered(1))


def _v_row(h, dt):
    return dt * N_HEADS + h


def _mem_attention(qm, gm, mk_ref, mv_ref, nb, rows, cat_scr, col0):
    row_view = len(mk_ref.shape) == 3
    for hh in range(N_HEADS):
        cs = slice(hh * MEM_HD, (hh + 1) * MEM_HD)
        if row_view:
            hsel = pl.ds(hh, mk_ref.shape[1] // N_HEADS, stride=N_HEADS)
            mk, mv = mk_ref[nb, hsel, :].astype(BF16), mv_ref[nb, hsel, :].astype(BF16)
        else:
            mk, mv = mk_ref[nb, hh], mv_ref[nb, hh]
        q = (qm[rows, cs] * (MEM_HD ** -0.5 * LOG2E)).astype(BF16)
        s = _dot_nt(q, mk)
        p = jnp.exp2(s - jnp.max(s, axis=-1, keepdims=True))
        l = jnp.sum(p, axis=-1, keepdims=True)
        o = _dot(p.astype(BF16), mv) / l
        cat_scr[rows, col0 + hh * MEM_HD: col0 + (hh + 1) * MEM_HD] = (o * _silu(gm[rows, cs])).astype(BF16)


def _memkv_kernel(mem_ref, g_ref, w_ref, mk_ref, mv_ref, mkb_ref, mvb_ref):
    tm = mem_ref.shape[0]
    base = _rms(mem_ref[...])
    nbm, M = mkb_ref.shape[1], mkb_ref.shape[3]
    for l in range(w_ref.shape[0]):
        kv = _dot((base * g_ref[l]).astype(BF16), w_ref[l])
        mk, mv = kv[:, :MEM_W], kv[:, MEM_W:]
        for hh in range(N_HEADS):
            cs = slice(hh * MEM_HD, (hh + 1) * MEM_HD)
            mk_ref[l, pl.ds(hh, tm, stride=N_HEADS), :] = mk[:, cs]
            mv_ref[l, pl.ds(hh, tm, stride=N_HEADS), :] = mv[:, cs]
            for bb in range(nbm):
                mkb_ref[l, bb, hh] = mk[bb * M:(bb + 1) * M, cs].astype(BF16)
                mvb_ref[l, bb, hh] = mv[bb * M:(bb + 1) * M, cs].astype(BF16)


def _memory_kv(mem, norm_mem, w_mem_kv):
    B, M, D = mem.shape
    nl = norm_mem.shape[0]
    rows = B * M
    tm = 512 if rows % 512 == 0 else M
    f = pl.pallas_call(
        _memkv_kernel,
        grid=(rows // tm,),
        in_specs=[
            pl.BlockSpec((tm, D), lambda i: (i, 0)),
            _const_spec((nl, 1, D)),
            _const_spec((nl, D, 2 * MEM_W)),
        ],
        out_specs=[pl.BlockSpec((nl, tm * N_HEADS, MEM_HD), lambda i: (0, i, 0))] * 2
        + [pl.BlockSpec((nl, tm // M, N_HEADS, M, MEM_HD), lambda i: (0, i, 0, 0, 0))] * 2,
        out_shape=[jax.ShapeDtypeStruct((nl, rows * N_HEADS, MEM_HD), F32)] * 2
        + [jax.ShapeDtypeStruct((nl, B, N_HEADS, M, MEM_HD), BF16)] * 2,
        compiler_params=pltpu.CompilerParams(
            dimension_semantics=("arbitrary",), vmem_limit_bytes=VMEM_LIMIT_BYTES),
        name="memkv",
    )
    mk, mv, mkb, mvb = f(mem.reshape(rows, D), norm_mem.reshape(nl, 1, D), w_mem_kv.astype(BF16))
    shp5 = (nl, B, M, N_HEADS, MEM_HD)
    return mk.reshape(shp5), mv.reshape(shp5), mkb, mvb


def _layer_a_kernel(*refs, NB, TT, L, has_state):
    it = iter(refs)
    x_ref, cos_ref, sin_ref = next(it), next(it), next(it)
    s0_ref = next(it) if has_state else None
    mk_ref, mv_ref, na_ref, win_ref, wout_ref, nkv_ref, wkv_ref = (next(it) for _ in range(7))
    h1_ref, k_ref, v_ref, kb_ref, vb_ref, S_ref = (next(it) for _ in range(6))
    h_scr, cat_scr, dec_scr = (next(it) for _ in range(3))

    R = NB * TT
    D = x_ref.shape[-1]
    QW = N_HEADS * HEAD_W
    t = pl.program_id(1)
    log_g = [math.log(1.0 - 2.0 ** (-5.0 - h)) for h in range(N_HEADS)]

    @pl.when((pl.program_id(0) == 0) & (t == 0))
    def _():
        d = (lax.broadcasted_iota(jnp.int32, (L, L), 0) - lax.broadcasted_iota(jnp.int32, (L, L), 1)).astype(F32)
        for h in range(N_HEADS):
            dec_scr[h] = jnp.where(d >= 0, jnp.exp(log_g[h] * jnp.maximum(d, 0.0)), 0.0)

    @pl.when(t == 0)
    def _():
        if has_state:
            S_ref[...] = s0_ref[...]
        else:
            S_ref[...] = jnp.zeros(S_ref.shape, F32)

    x = x_ref[...].reshape(R, D)
    h_scr[...] = (_rms(x) * na_ref[...]).astype(BF16)
    cos = cos_ref[...]
    sin = sin_ref[...]
    half = HEAD_W // 2
    idx = lax.broadcasted_iota(jnp.int32, (L, 1), 0).astype(F32)

    def rope(c0):
        y = _dot(h_scr[...], win_ref[:, c0:c0 + HEAD_W])
        y1, y2 = y[:, :half], y[:, half:]
        return jnp.concatenate([y1 * cos - y2 * sin, y2 * cos + y1 * sin], axis=-1)

    for h in range(N_HEADS):
        q = rope(h * HEAD_W).astype(BF16)
        k = rope(QW + h * HEAD_W) * (HEAD_W ** -0.5)
        v = _dot(h_scr[...], win_ref[:, 2 * QW + h * HEAD_W: 2 * QW + (h + 1) * HEAD_W]).astype(BF16)
        g = _dot(h_scr[...], win_ref[:, 3 * QW + h * HEAD_W: 3 * QW + (h + 1) * HEAD_W])
        a_in = jnp.exp(log_g[h] * (idx + 1.0))
        b_out = jnp.exp(log_g[h] * (L - 1.0 - idx))
        g_blk = math.exp(log_g[h] * L)
        for nb in range(NB):
            for j in range(TT // L):
                rows = slice(nb * TT + j * L, nb * TT + (j + 1) * L)
                qb, vb = q[rows], v[rows]
                kf = k[rows]
                S = S_ref[nb, h]
                sc = _dot_nt(qb, kf.astype(BF16)) * dec_scr[h]
                o = _dot(sc.astype(BF16), vb) + _dot(qb, S.astype(BF16)) * a_in
                S_ref[nb, h] = g_blk * S + _dot_tn((kf * b_out).astype(BF16), vb)
                cat_scr[rows, h * HEAD_W:(h + 1) * HEAD_W] = (_rms(o) * _silu(g[rows])).astype(BF16)

    qm = _dot(h_scr[...], win_ref[:, 4 * QW: 4 * QW + MEM_W])
    gm = _dot(h_scr[...], win_ref[:, 4 * QW + MEM_W: 4 * QW + 2 * MEM_W])
    for nb in range(NB):
        _mem_attention(qm, gm, mk_ref, mv_ref, nb, slice(nb * TT, (nb + 1) * TT), cat_scr, QW)

    NC = 512
    for c in range(D // NC):
        cs = slice(c * NC, (c + 1) * NC)
        y = _dot(cat_scr[...], wout_ref[:, cs]) + x_ref[:, :, cs].reshape(R, NC)
        h1_ref[:, :, cs] = y.reshape(NB, TT, NC)

    hk = (_rms(h1_ref[...].reshape(R, D)) * nkv_ref[...]).astype(BF16)
    KW = N_HEADS * HEAD_W
    for c in range(2 * KW // NC):
        kv = _dot(hk, wkv_ref[:, c * NC:(c + 1) * NC])
        is_k = c < KW // NC
        col0 = c * NC if is_k else c * NC - KW
        for hl in range(NC // HEAD_W):
            (kb_ref if is_k else vb_ref)[:, col0 // HEAD_W + hl] = (
                kv[:, hl * HEAD_W:(hl + 1) * HEAD_W].astype(BF16).reshape(NB, TT, HEAD_W))
        for jj in range(NC // LANES):
            col = col0 + jj * LANES
            if is_k:
                row = col // LANES
                k_ref[pl.ds(row, R, stride=KV_GROUPS), :] = kv[:, jj * LANES:(jj + 1) * LANES]
            else:
                row = _v_row(col // HEAD_W, (col % HEAD_W) // LANES)
                v_ref[pl.ds(row, R, stride=KV_GROUPS), :] = kv[:, jj * LANES:(jj + 1) * LANES]


def _layer_a(x, cos, sin, s0, mkb, mvb, norm_a, w_in, w_out, norm_kv, w_kv, *, NB, TT, L, mem_layer):
    B, T, D = x.shape
    R = NB * TT
    nt = T // TT
    assert NB == 1 or nt == 1
    KW = w_kv.shape[1] // 2
    has_state = s0 is not None
    bt = lambda b, t: (b, t, 0)
    b0 = lambda b, t: (b, 0, 0)
    in_specs = [pl.BlockSpec((NB, TT, D), bt),
                pl.BlockSpec((R, DQK), lambda b, t: (t, 0)),
                pl.BlockSpec((R, DQK), lambda b, t: (t, 0))]
    args = [x, cos, sin]
    if has_state:
        in_specs.append(pl.BlockSpec((NB, N_HEADS, HEAD_W, HEAD_W), lambda b, t: (b, 0, 0, 0)))
        args.append(s0)
    in_specs += [_mem_spec(mkb, mem_layer, NB), _mem_spec(mvb, mem_layer, NB),
                 _const_spec((1, D)), _const_spec(w_in.shape), _const_spec(w_out.shape),
                 _const_spec((1, D)), _const_spec(w_kv.shape)]
    args += [mkb, mvb, norm_a.reshape(1, D), w_in, w_out, norm_kv.reshape(1, D), w_kv]
    rows_spec = pl.BlockSpec((R * KV_GROUPS, LANES), lambda b, t: (b * nt + t, 0))
    assert KW == N_HEADS * HEAD_W
    hm_spec = pl.BlockSpec((NB, N_HEADS, TT, HEAD_W), lambda b, t: (b, 0, t, 0))
    out_specs = [pl.BlockSpec((NB, TT, D), bt), rows_spec, rows_spec, hm_spec, hm_spec,
                 pl.BlockSpec((NB, N_HEADS, HEAD_W, HEAD_W), lambda b, t: (b, 0, 0, 0))]
    out_shape = [jax.ShapeDtypeStruct((B, T, D), F32)] + [
        jax.ShapeDtypeStruct((B * T * KV_GROUPS, LANES), F32)] * 2 + [
        jax.ShapeDtypeStruct((B, N_HEADS, T, HEAD_W), BF16)] * 2 + [
        jax.ShapeDtypeStruct((B, N_HEADS, HEAD_W, HEAD_W), F32)]
    f = pl.pallas_call(
        functools.partial(_layer_a_kernel, NB=NB, TT=TT, L=L, has_state=has_state),
        grid=(B // NB, nt),
        in_specs=in_specs,
        out_specs=out_specs,
        out_shape=out_shape,
        scratch_shapes=[pltpu.VMEM((R, D), BF16), pltpu.VMEM((R, N_HEADS * HEAD_W + MEM_W), BF16),
                        pltpu.VMEM((N_HEADS, L, L), F32)],
        compiler_params=pltpu.CompilerParams(
            dimension_semantics=("arbitrary", "arbitrary"), vmem_limit_bytes=VMEM_LIMIT_BYTES),
        name="layer_a",
    )
    return f(*args)


def _k_view_to_5d(kr, B, T):
    return kr.reshape(B, T, N_HEADS, 2, DQK)


def _v_view_to_4d(vr, B, T):
    return vr.reshape(B, T, 2, N_HEADS, LANES).transpose(0, 1, 3, 2, 4).reshape(B, T, N_HEADS, HEAD_W)


def _diff_lambda(lp_ref, lam_init):
    lp = lp_ref[...]
    a = jnp.sum(lp[0:1] * lp[1:2], axis=-1, keepdims=True)
    b = jnp.sum(lp[2:3] * lp[3:4], axis=-1, keepdims=True)
    return jnp.exp(a) - jnp.exp(b) + lam_init


def _diff_project_q(h1, nb_ref, win_ref, h_scr, q_scr):
    h_scr[...] = (_rms(h1) * nb_ref[...]).astype(BF16)
    for hh in range(N_HEADS):
        q = _dot(h_scr[...], win_ref[:, hh * HEAD_W:(hh + 1) * HEAD_W]) * (DQK ** -0.5 * LOG2E)
        q_scr[hh] = q.astype(BF16)


def _diff_finish(h1, o_of, lam_init, h_scr, win_ref, sub_ref, mk_ref, mv_ref, wout_ref, nf_ref, cat_scr, y_ref):
    TQ, D = h1.shape
    QW = N_HEADS * HEAD_W
    for hh in range(N_HEADS):
        o = _rms(o_of(hh)) * sub_ref[...] * (1.0 - lam_init)
        g = _dot(h_scr[...], win_ref[:, QW + hh * HEAD_W: QW + (hh + 1) * HEAD_W])
        cat_scr[:, hh * HEAD_W:(hh + 1) * HEAD_W] = (o * _silu(g)).astype(BF16)
    qm = _dot(h_scr[...], win_ref[:, 2 * QW: 2 * QW + MEM_W])
    gm = _dot(h_scr[...], win_ref[:, 2 * QW + MEM_W: 2 * QW + 2 * MEM_W])
    _mem_attention(qm, gm, mk_ref, mv_ref, 0, slice(0, TQ), cat_scr, QW)
    NC = 512
    for c in range(D // NC):
        cs = slice(c * NC, (c + 1) * NC)
        y_ref[0, :, cs] = _dot(cat_scr[...], wout_ref[:, cs]) + h1[:, cs]
    y_ref[0] = _rms(y_ref[0]) * nf_ref[...]


def _layer_b_prompt_kernel(h1_ref, kb_ref, vb_ref, mk_ref, mv_ref, nb_ref, win_ref, lp_ref, sub_ref, wout_ref,
                           nf_ref, y_ref, h_scr, q_scr, o_scr, cat_scr, *, TQ, SUB, NQ, lam_init):
    qi = pl.program_id(1)
    h1 = h1_ref[0]
    _diff_project_q(h1, nb_ref, win_ref, h_scr, q_scr)
    lam = _diff_lambda(lp_ref, lam_init)
    diag = (lax.broadcasted_iota(jnp.int32, (SUB, SUB), 1) // CHUNK) <= (
        lax.broadcasted_iota(jnp.int32, (SUB, SUB), 0) // CHUNK)

    for case in range(NQ):
        @pl.when(qi == case)
        def _(case=case):
            for sub in range(TQ // SUB):
                rows = slice(sub * SUB, (sub + 1) * SUB)
                kt = case * TQ + (sub + 1) * SUB
                for hh in range(N_HEADS):
                    ps, ls = [], []
                    for c in range(2):
                        s = _dot_nt(q_scr[hh, rows, c * DQK:(c + 1) * DQK],
                                    kb_ref[0, hh, :kt, c * DQK:(c + 1) * DQK])
                        last = jnp.where(diag, s[:, kt - SUB:], NEG_BIG)
                        s = last if kt == SUB else jnp.concatenate([s[:, :kt - SUB], last], axis=1)
                        p = jnp.exp2(s - jnp.max(s, axis=-1, keepdims=True))
                        ls.append(jnp.sum(p, axis=-1, keepdims=True))
                        ps.append(p.astype(BF16))
                    v = vb_ref[0, hh, :kt, :]
                    o_scr[hh, rows] = _dot(ps[0], v) / ls[0] - lam * (_dot(ps[1], v) / ls[1])

    _diff_finish(h1, lambda hh: o_scr[hh], lam_init, h_scr, win_ref, sub_ref, mk_ref, mv_ref, wout_ref, nf_ref,
                 cat_scr, y_ref)


def _layer_b_prompt(h1, kb, vb, mkb, mvb, norm_b, w_in, lam_p, subln, w_out, norm_f, *, TQ, SUB, lam_init,
                    mem_layer):
    B, T, D = h1.shape
    kv_spec = pl.BlockSpec((1, N_HEADS, T, HEAD_W), lambda b, q: (b, 0, 0, 0))
    f = pl.pallas_call(
        functools.partial(_layer_b_prompt_kernel, TQ=TQ, SUB=SUB, NQ=T // TQ, lam_init=lam_init),
        grid=(B, T // TQ),
        in_specs=[pl.BlockSpec((1, TQ, D), lambda b, q: (b, q, 0)), kv_spec, kv_spec,
                  _mem_spec(mkb, mem_layer, 1), _mem_spec(mvb, mem_layer, 1),
                  _const_spec((1, D)), _const_spec(w_in.shape), _const_spec(lam_p.shape),
                  _const_spec((1, HEAD_W)), _const_spec(w_out.shape), _const_spec((1, D))],
        out_specs=pl.BlockSpec((1, TQ, D), lambda b, q: (b, q, 0)),
        out_shape=jax.ShapeDtypeStruct((B, T, D), F32),
        scratch_shapes=[pltpu.VMEM((TQ, D), BF16), pltpu.VMEM((N_HEADS, TQ, HEAD_W), BF16),
                        pltpu.VMEM((N_HEADS, TQ, HEAD_W), F32),
                        pltpu.VMEM((TQ, N_HEADS * HEAD_W + MEM_W), BF16)],
        compiler_params=pltpu.CompilerParams(
            dimension_semantics=("arbitrary", "arbitrary"), vmem_limit_bytes=VMEM_LIMIT_BYTES),
        name="layer_b_prompt",
    )
    return f(h1, kb, vb, mkb, mvb, norm_b.reshape(1, D), w_in, lam_p, subln.reshape(1, HEAD_W), w_out,
             norm_f.reshape(1, D))


def _online_tile(q_scr, k_of, v_of, m_scr, l_scr, acc_scr):
    TQ = q_scr.shape[1]
    for hh in range(N_HEADS):
        ps, alphas = [], []
        for c in range(2):
            i = 2 * hh + c
            s = _dot_nt(q_scr[hh, :, c * DQK:(c + 1) * DQK], k_of(hh, c))
            m_prev = m_scr[i]
            m_new = jnp.maximum(m_prev, jnp.max(s, axis=-1, keepdims=True))
            alpha = jnp.exp2(m_prev - m_new)
            p = jnp.exp2(s - m_new)
            l_scr[i] = alpha * l_scr[i] + jnp.sum(p, axis=-1, keepdims=True)
            m_scr[i] = m_new
            ps.append(p.astype(BF16))
            alphas.append(alpha)
        pv = _dot(jnp.concatenate(ps, axis=0), v_of(hh))
        acc_scr[2 * hh] = alphas[0] * acc_scr[2 * hh] + pv[:TQ]
        acc_scr[2 * hh + 1] = alphas[1] * acc_scr[2 * hh + 1] + pv[TQ:]


def _layer_b_sample_kernel(h1_ref, ck_ref, cv_ref, kn_ref, vn_ref, mk_ref, mv_ref, nb_ref, win_ref, lp_ref, sub_ref,
                           wout_ref, nf_ref, y_ref, h_scr, q_scr, m_scr, l_scr, acc_scr, cat_scr, *, TK, lam_init):
    kt = pl.program_id(1)

    @pl.when(kt == 0)
    def _():
        _diff_project_q(h1_ref[0], nb_ref, win_ref, h_scr, q_scr)
        m_scr[...] = jnp.full(m_scr.shape, NEG_BIG, F32)
        l_scr[...] = jnp.zeros(l_scr.shape, F32)
        acc_scr[...] = jnp.zeros(acc_scr.shape, F32)

    def ck_of(hh, c):
        return ck_ref[pl.ds(2 * hh + c, TK, stride=KV_GROUPS), :].astype(BF16)

    def cv_of(hh):
        return jnp.concatenate([cv_ref[pl.ds(_v_row(hh, dt), TK, stride=KV_GROUPS), :] for dt in range(2)],
                               axis=1).astype(BF16)

    _online_tile(q_scr, ck_of, cv_of, m_scr, l_scr, acc_scr)

    @pl.when(kt == pl.num_programs(1) - 1)
    def _():
        _online_tile(q_scr,
                     lambda hh, c: kn_ref[0, hh, :, c * DQK:(c + 1) * DQK],
                     lambda hh: vn_ref[0, hh],
                     m_scr, l_scr, acc_scr)
        lam = _diff_lambda(lp_ref, lam_init)
        o_of = lambda hh: (acc_scr[2 * hh] / l_scr[2 * hh]
                           - lam * (acc_scr[2 * hh + 1] / l_scr[2 * hh + 1]))
        _diff_finish(h1_ref[0], o_of, lam_init, h_scr, win_ref, sub_ref, mk_ref, mv_ref, wout_ref, nf_ref,
                     cat_scr, y_ref)


def _layer_b_sample(h1, ck_rows, cv_rows, kb, vb, mkb, mvb, norm_b, w_in, lam_p, subln, w_out, norm_f, *, P, TK,
                    lam_init, mem_layer):
    B, T, D = h1.shape
    nkt = P // TK
    b0 = lambda b, k: (b, 0, 0)
    cache_spec = pl.BlockSpec((TK * KV_GROUPS, LANES), lambda b, k: (b * nkt + k, 0))
    new_spec = pl.BlockSpec((1, N_HEADS, T, HEAD_W), lambda b, k: (b, 0, 0, 0))
    f = pl.pallas_call(
        functools.partial(_layer_b_sample_kernel, TK=TK, lam_init=lam_init),
        grid=(B, nkt),
        in_specs=[pl.BlockSpec((1, T, D), b0), cache_spec, cache_spec, new_spec, new_spec,
                  _mem_spec(mkb, mem_layer, 1), _mem_spec(mvb, mem_layer, 1),
                  _const_spec((1, D)), _const_spec(w_in.shape), _const_spec(lam_p.shape),
                  _const_spec((1, HEAD_W)), _const_spec(w_out.shape), _const_spec((1, D))],
        out_specs=pl.BlockSpec((1, T, D), b0),
        out_shape=jax.ShapeDtypeStruct((B, T, D), F32),
        scratch_shapes=[pltpu.VMEM((T, D), BF16), pltpu.VMEM((N_HEADS, T, HEAD_W), BF16),
                        pltpu.VMEM((2 * N_HEADS, T, 1), F32), pltpu.VMEM((2 * N_HEADS, T, 1), F32),
                        pltpu.VMEM((2 * N_HEADS, T, HEAD_W), F32),
                        pltpu.VMEM((T, N_HEADS * HEAD_W + MEM_W), BF16)],
        compiler_params=pltpu.CompilerParams(
            dimension_semantics=("arbitrary", "arbitrary"), vmem_limit_bytes=VMEM_LIMIT_BYTES),
        name="layer_b_sample",
    )
    return f(h1, ck_rows, cv_rows, kb, vb, mkb, mvb, norm_b.reshape(1, D), w_in, lam_p, subln.reshape(1, HEAD_W),
             w_out, norm_f.reshape(1, D))


def _rope_tables(pos):
    half = DQK
    inv = ROPE_BASE ** (-jnp.arange(half, dtype=F32) / half)
    ang = pos.astype(F32)[:, None] * inv[None, :]
    return jnp.cos(ang), jnp.sin(ang)


def _pick(n, candidates):
    for c in candidates:
        if n % c == 0:
            return c
    return n


def kernel(x_prompt, x_sample, state_ret, cache_k, cache_v, cache_mem_k, cache_mem_v, mem_prompt,
           norm_a, w_in_a, w_out_a, norm_kv, w_kv, norm_b, w_in_b, diff_lambda, subln_b, w_out_b,
           norm_mem, w_mem_kv, norm_f):
    depth = norm_mem.shape[0]
    assert norm_a.shape[0] == 1 and norm_b.shape[0] == 1 and depth == 2, "kernel is written for depth 2"
    lam_init = 0.8 - 0.6 * math.exp(-0.3 * 1)
    wia, woa, wkv = w_in_a[0].astype(BF16), w_out_a[0].astype(BF16), w_kv.astype(BF16)
    wib, wob = w_in_b[0].astype(BF16), w_out_b[0].astype(BF16)

    Bp, Tp, D = x_prompt.shape
    mk_p, mv_p, mkb_p, mvb_p = _memory_kv(mem_prompt, norm_mem, w_mem_kv)
    cos_p, sin_p = _rope_tables(jnp.arange(Tp))
    blk = _pick(Tp, (256, 128, CHUNK))
    tt = _pick(Tp, (2 * blk, blk))
    h1_p, k_p, v_p, kb_p, vb_p, S_p = _layer_a(
        x_prompt, cos_p, sin_p, None, mkb_p, mvb_p, norm_a[0], wia, woa, norm_kv, wkv, NB=1, TT=tt, L=blk,
        mem_layer=0)
    y_p = _layer_b_prompt(h1_p, kb_p, vb_p, mkb_p, mvb_p, norm_b[0], wib, diff_lambda[0], subln_b[0], wob,
                          norm_f, TQ=blk, SUB=blk, lam_init=lam_init, mem_layer=1)

    Bs, Ts, _ = x_sample.shape
    past = cache_k.shape[1]
    nbs = _pick(Bs, (4, 2, 1))
    cos_s, sin_s = _rope_tables(past + jnp.arange(Ts))
    cos_s, sin_s = jnp.tile(cos_s, (nbs, 1)), jnp.tile(sin_s, (nbs, 1))
    M = cache_mem_k.shape[2]
    cmk = cache_mem_k.reshape(depth, Bs, M * N_HEADS, MEM_HD)
    cmv = cache_mem_v.reshape(depth, Bs, M * N_HEADS, MEM_HD)
    h1_s, k_s, v_s, kb_s, vb_s, S_s = _layer_a(
        x_sample, cos_s, sin_s, state_ret[0], cmk, cmv, norm_a[0], wia, woa, norm_kv, wkv,
        NB=nbs, TT=Ts, L=Ts, mem_layer=0)
    ck_rows = cache_k.reshape(Bs * past * KV_GROUPS, LANES)
    cv_rows = cache_v.reshape(Bs, past, N_HEADS, 2, LANES).transpose(0, 1, 3, 2, 4).reshape(
        Bs * past * KV_GROUPS, LANES)
    y_s = _layer_b_sample(h1_s, ck_rows, cv_rows, kb_s, vb_s, cmk, cmv, norm_b[0], wib, diff_lambda[0],
                          subln_b[0], wob, norm_f, P=past, TK=_pick(past, (2048, 1024, 512, 256, 128)), lam_init=lam_init,
                          mem_layer=1)

    return (y_p, y_s, S_p[None], _k_view_to_5d(k_p, Bp, Tp), _v_view_to_4d(v_p, Bp, Tp),
            mk_p, mv_p, S_s[None], _k_view_to_5d(k_s, Bs, Ts), _v_view_to_4d(v_s, Bs, Ts))
```

```python
import functools
import math

import jax
import jax.numpy as jnp
from jax import lax
from jax.experimental import pallas as pl
from jax.experimental.pallas import tpu as pltpu

F32 = jnp.float32
BF16 = jnp.bfloat16

EPS = 1e-6
ROPE_BASE = 10000.0
CHUNK = 64
N_HEADS = 4
HEAD_W = 256
DQK = 128
MEM_HD = 128
MEM_W = N_HEADS * MEM_HD
LANES = 128
KV_GROUPS = N_HEADS * HEAD_W // LANES
NEG_BIG = -1e30
LOG2E = math.log2(math.e)

VMEM_LIMIT_BYTES = 56 * 1024 * 1024


def _dot(a, b):
    return jnp.dot(a, b, preferred_element_type=F32)


def _dot_nt(a, b):
    return lax.dot_general(a, b, (((1,), (1,)), ((), ())), preferred_element_type=F32)


def _dot_tn(a, b):
    return lax.dot_general(a, b, (((0,), (0,)), ((), ())), preferred_element_type=F32)


def _rms(x):
    return x * lax.rsqrt(jnp.mean(x * x, axis=-1, keepdims=True) + EPS)


def _silu(g):
    return (0.5 * g) * (1.0 + jnp.tanh(0.5 * g))


def _mem_spec(arr, layer, nb):
    zeros = (0,) * (arr.ndim - 2)
    return pl.BlockSpec((None, nb) + arr.shape[2:], lambda b, t: (layer, b) + zeros)


def _const_spec(shape):
    nd = len(shape)
    return pl.BlockSpec(shape, lambda *_: (0,) * nd, pipeline_mode=pl.Buffered(1))


def _v_row(h, dt):
    return dt * N_HEADS + h


def _mem_attention(qm, gm, mk_ref, mv_ref, nb, rows, cat_scr, col0):
    row_view = len(mk_ref.shape) == 3
    for hh in range(N_HEADS):
        cs = slice(hh * MEM_HD, (hh + 1) * MEM_HD)
        if row_view:
            hsel = pl.ds(hh, mk_ref.shape[1] // N_HEADS, stride=N_HEADS)
            mk, mv = mk_ref[nb, hsel, :].astype(BF16), mv_ref[nb, hsel, :].astype(BF16)
        else:
            mk, mv = mk_ref[nb, hh], mv_ref[nb, hh]
        q = (qm[rows, cs] * (MEM_HD ** -0.5 * LOG2E)).astype(BF16)
        s = _dot_nt(q, mk)
        p = jnp.exp2(s - jnp.max(s, axis=-1, keepdims=True))
        l = jnp.sum(p, axis=-1, keepdims=True)
        o = _dot(p.astype(BF16), mv) / l
        cat_scr[rows, col0 + hh * MEM_HD: col0 + (hh + 1) * MEM_HD] = (o * _silu(gm[rows, cs])).astype(BF16)


def _memkv_kernel(mem_ref, g_ref, w_ref, mk_ref, mv_ref, mkb_ref, mvb_ref):
    tm = mem_ref.shape[0]
    base = _rms(mem_ref[...])
    nbm, M = mkb_ref.shape[1], mkb_ref.shape[3]
    for l in range(w_ref.shape[0]):
        kv = _dot((base * g_ref[l]).astype(BF16), w_ref[l])
        mk, mv = kv[:, :MEM_W], kv[:, MEM_W:]
        for hh in range(N_HEADS):
            cs = slice(hh * MEM_HD, (hh + 1) * MEM_HD)
            mk_ref[l, pl.ds(hh, tm, stride=N_HEADS), :] = mk[:, cs]
            mv_ref[l, pl.ds(hh, tm, stride=N_HEADS), :] = mv[:, cs]
            for bb in range(nbm):
                mkb_ref[l, bb, hh] = mk[bb * M:(bb + 1) * M, cs].astype(BF16)
                mvb_ref[l, bb, hh] = mv[bb * M:(bb + 1) * M, cs].astype(BF16)


def _memory_kv(mem, norm_mem, w_mem_kv):
    B, M, D = mem.shape
    nl = norm_mem.shape[0]
    rows = B * M
    tm = 512 if rows % 512 == 0 else M
    f = pl.pallas_call(
        _memkv_kernel,
        grid=(rows // tm,),
        in_specs=[
            pl.BlockSpec((tm, D), lambda i: (i, 0)),
            _const_spec((nl, 1, D)),
            _const_spec((nl, D, 2 * MEM_W)),
        ],
        out_specs=[pl.BlockSpec((nl, tm * N_HEADS, MEM_HD), lambda i: (0, i, 0))] * 2
        + [pl.BlockSpec((nl, tm // M, N_HEADS, M, MEM_HD), lambda i: (0, i, 0, 0, 0))] * 2,
        out_shape=[jax.ShapeDtypeStruct((nl, rows * N_HEADS, MEM_HD), F32)] * 2
        + [jax.ShapeDtypeStruct((nl, B, N_HEADS, M, MEM_HD), BF16)] * 2,
        compiler_params=pltpu.CompilerParams(
            dimension_semantics=("arbitrary",), vmem_limit_bytes=VMEM_LIMIT_BYTES),
        name="memkv",
    )
    mk, mv, mkb, mvb = f(mem.reshape(rows, D), norm_mem.reshape(nl, 1, D), w_mem_kv.astype(BF16))
    shp5 = (nl, B, M, N_HEADS, MEM_HD)
    return mk.reshape(shp5), mv.reshape(shp5), mkb, mvb


def _layer_a_kernel(*refs, NB, TT, L, has_state):
    it = iter(refs)
    x_ref, cos_ref, sin_ref = next(it), next(it), next(it)
    s0_ref = next(it) if has_state else None
    mk_ref, mv_ref, na_ref, win_ref, wout_ref, nkv_ref, wkv_ref = (next(it) for _ in range(7))
    h1_ref, k_ref, v_ref, kb_ref, vb_ref, S_ref = (next(it) for _ in range(6))
    h_scr, cat_scr, dec_scr = (next(it) for _ in range(3))

    R = NB * TT
    D = x_ref.shape[-1]
    QW = N_HEADS * HEAD_W
    t = pl.program_id(1)
    log_g = [math.log(1.0 - 2.0 ** (-5.0 - h)) for h in range(N_HEADS)]

    @pl.when((pl.program_id(0) == 0) & (t == 0))
    def _():
        d = (lax.broadcasted_iota(jnp.int32, (L, L), 0) - lax.broadcasted_iota(jnp.int32, (L, L), 1)).astype(F32)
        for h in range(N_HEADS):
            dec_scr[h] = jnp.where(d >= 0, jnp.exp(log_g[h] * jnp.maximum(d, 0.0)), 0.0)

    @pl.when(t == 0)
    def _():
        if has_state:
            S_ref[...] = s0_ref[...]
        else:
            S_ref[...] = jnp.zeros(S_ref.shape, F32)

    x = x_ref[...].reshape(R, D)
    h_scr[...] = (_rms(x) * na_ref[...]).astype(BF16)
    cos = cos_ref[...]
    sin = sin_ref[...]
    half = HEAD_W // 2
    idx = lax.broadcasted_iota(jnp.int32, (L, 1), 0).astype(F32)

    def rope(c0):
        y = _dot(h_scr[...], win_ref[:, c0:c0 + HEAD_W])
        y1, y2 = y[:, :half], y[:, half:]
        return jnp.concatenate([y1 * cos - y2 * sin, y2 * cos + y1 * sin], axis=-1)

    for h in range(N_HEADS):
        q = rope(h * HEAD_W).astype(BF16)
        k = rope(QW + h * HEAD_W) * (HEAD_W ** -0.5)
        v = _dot(h_scr[...], win_ref[:, 2 * QW + h * HEAD_W: 2 * QW + (h + 1) * HEAD_W]).astype(BF16)
        g = _dot(h_scr[...], win_ref[:, 3 * QW + h * HEAD_W: 3 * QW + (h + 1) * HEAD_W])
        a_in = jnp.exp(log_g[h] * (idx + 1.0))
        b_out = jnp.exp(log_g[h] * (L - 1.0 - idx))
        g_blk = math.exp(log_g[h] * L)
        for nb in range(NB):
            for j in range(TT // L):
                rows = slice(nb * TT + j * L, nb * TT + (j + 1) * L)
                qb, vb = q[rows], v[rows]
                kf = k[rows]
                S = S_ref[nb, h]
                sc = _dot_nt(qb, kf.astype(BF16)) * dec_scr[h]
                o = _dot(sc.astype(BF16), vb) + _dot(qb, S.astype(BF16)) * a_in
                S_ref[nb, h] = g_blk * S + _dot_tn((kf * b_out).astype(BF16), vb)
                cat_scr[rows, h * HEAD_W:(h + 1) * HEAD_W] = (_rms(o) * _silu(g[rows])).astype(BF16)

    qm = _dot(h_scr[...], win_ref[:, 4 * QW: 4 * QW + MEM_W])
    gm = _dot(h_scr[...], win_ref[:, 4 * QW + MEM_W: 4 * QW + 2 * MEM_W])
    for nb in range(NB):
        _mem_attention(qm, gm, mk_ref, mv_ref, nb, slice(nb * TT, (nb + 1) * TT), cat_scr, QW)

    NC = 512
    for c in range(D // NC):
        cs = slice(c * NC, (c + 1) * NC)
        y = _dot(cat_scr[...], wout_ref[:, cs]) + x_ref[:, :, cs].reshape(R, NC)
        h1_ref[:, :, cs] = y.reshape(NB, TT, NC)

    hk = (_rms(h1_ref[...].reshape(R, D)) * nkv_ref[...]).astype(BF16)
    KW = N_HEADS * HEAD_W
    for c in range(2 * KW // NC):
        kv = _dot(hk, wkv_ref[:, c * NC:(c + 1) * NC])
        is_k = c < KW // NC
        col0 = c * NC if is_k else c * NC - KW
        for hl in range(NC // HEAD_W):
            (kb_ref if is_k else vb_ref)[:, col0 // HEAD_W + hl] = (
                kv[:, hl * HEAD_W:(hl + 1) * HEAD_W].astype(BF16).reshape(NB, TT, HEAD_W))
        for jj in range(NC // LANES):
            col = col0 + jj * LANES
            if is_k:
                row = col // LANES
                k_ref[pl.ds(row, R, stride=KV_GROUPS), :] = kv[:, jj * LANES:(jj + 1) * LANES]
            else:
                row = _v_row(col // HEAD_W, (col % HEAD_W) // LANES)
                v_ref[pl.ds(row, R, stride=KV_GROUPS), :] = kv[:, jj * LANES:(jj + 1) * LANES]


def _layer_a(x, cos, sin, s0, mkb, mvb, norm_a, w_in, w_out, norm_kv, w_kv, *, NB, TT, L, mem_layer):
    B, T, D = x.shape
    R = NB * TT
    nt = T // TT
    assert NB == 1 or nt == 1
    KW = w_kv.shape[1] // 2
    has_state = s0 is not None
    bt = lambda b, t: (b, t, 0)
    b0 = lambda b, t: (b, 0, 0)
    in_specs = [pl.BlockSpec((NB, TT, D), bt),
                pl.BlockSpec((R, DQK), lambda b, t: (t, 0)),
                pl.BlockSpec((R, DQK), lambda b, t: (t, 0))]
    args = [x, cos, sin]
    if has_state:
        in_specs.append(pl.BlockSpec((NB, N_HEADS, HEAD_W, HEAD_W), lambda b, t: (b, 0, 0, 0)))
        args.append(s0)
    in_specs += [_mem_spec(mkb, mem_layer, NB), _mem_spec(mvb, mem_layer, NB),
                 _const_spec((1, D)), _const_spec(w_in.shape), _const_spec(w_out.shape),
                 _const_spec((1, D)), _const_spec(w_kv.shape)]
    args += [mkb, mvb, norm_a.reshape(1, D), w_in, w_out, norm_kv.reshape(1, D), w_kv]
    rows_spec = pl.BlockSpec((R * KV_GROUPS, LANES), lambda b, t: (b * nt + t, 0))
    assert KW == N_HEADS * HEAD_W
    hm_spec = pl.BlockSpec((NB, N_HEADS, TT, HEAD_W), lambda b, t: (b, 0, t, 0))
    out_specs = [pl.BlockSpec((NB, TT, D), bt), rows_spec, rows_spec, hm_spec, hm_spec,
                 pl.BlockSpec((NB, N_HEADS, HEAD_W, HEAD_W), lambda b, t: (b, 0, 0, 0))]
    out_shape = [jax.ShapeDtypeStruct((B, T, D), F32)] + [
        jax.ShapeDtypeStruct((B * T * KV_GROUPS, LANES), F32)] * 2 + [
        jax.ShapeDtypeStruct((B, N_HEADS, T, HEAD_W), BF16)] * 2 + [
        jax.ShapeDtypeStruct((B, N_HEADS, HEAD_W, HEAD_W), F32)]
    f = pl.pallas_call(
        functools.partial(_layer_a_kernel, NB=NB, TT=TT, L=L, has_state=has_state),
        grid=(B // NB, nt),
        in_specs=in_specs,
        out_specs=out_specs,
        out_shape=out_shape,
        scratch_shapes=[pltpu.VMEM((R, D), BF16), pltpu.VMEM((R, N_HEADS * HEAD_W + MEM_W), BF16),
                        pltpu.VMEM((N_HEADS, L, L), F32)],
        compiler_params=pltpu.CompilerParams(
            dimension_semantics=("arbitrary", "arbitrary"), vmem_limit_bytes=VMEM_LIMIT_BYTES),
        name="layer_a",
    )
    return f(*args)


def _k_view_to_5d(kr, B, T):
    return kr.reshape(B, T, N_HEADS, 2, DQK)


def _v_view_to_4d(vr, B, T):
    return vr.reshape(B, T, 2, N_HEADS, LANES).transpose(0, 1, 3, 2, 4).reshape(B, T, N_HEADS, HEAD_W)


def _diff_lambda(lp_ref, lam_init):
    lp = lp_ref[...]
    a = jnp.sum(lp[0:1] * lp[1:2], axis=-1, keepdims=True)
    b = jnp.sum(lp[2:3] * lp[3:4], axis=-1, keepdims=True)
    return jnp.exp(a) - jnp.exp(b) + lam_init


def _diff_project_q(h1, nb_ref, win_ref, h_scr, q_scr):
    h_scr[...] = (_rms(h1) * nb_ref[...]).astype(BF16)
    for hh in range(N_HEADS):
        q = _dot(h_scr[...], win_ref[:, hh * HEAD_W:(hh + 1) * HEAD_W]) * (DQK ** -0.5 * LOG2E)
        q_scr[hh] = q.astype(BF16)


def _diff_finish(h1, o_of, lam_init, h_scr, win_ref, sub_ref, mk_ref, mv_ref, wout_ref, nf_ref, cat_scr, y_ref, NB=1):
    R, D = h1.shape
    T = R // NB
    QW = N_HEADS * HEAD_W
    for hh in range(N_HEADS):
        o = _rms(o_of(hh)) * sub_ref[...] * (1.0 - lam_init)
        g = _dot(h_scr[...], win_ref[:, QW + hh * HEAD_W: QW + (hh + 1) * HEAD_W])
        cat_scr[:, hh * HEAD_W:(hh + 1) * HEAD_W] = (o * _silu(g)).astype(BF16)
    qm = _dot(h_scr[...], win_ref[:, 2 * QW: 2 * QW + MEM_W])
    gm = _dot(h_scr[...], win_ref[:, 2 * QW + MEM_W: 2 * QW + 2 * MEM_W])
    for nb in range(NB):
        _mem_attention(qm, gm, mk_ref, mv_ref, nb, slice(nb * T, (nb + 1) * T), cat_scr, QW)
    NC = 512
    for c in range(D // NC):
        cs = slice(c * NC, (c + 1) * NC)
        y_ref[:, :, cs] = (_dot(cat_scr[...], wout_ref[:, cs]) + h1[:, cs]).reshape(NB, T, NC)
    y_ref[...] = (_rms(y_ref[...].reshape(R, D)) * nf_ref[...]).reshape(NB, T, D)


def _layer_b_prompt_kernel(h1_ref, kb_ref, vb_ref, mk_ref, mv_ref, nb_ref, win_ref, lp_ref, sub_ref, wout_ref,
                           nf_ref, y_ref, h_scr, q_scr, o_scr, cat_scr, *, TQ, SUB, NQ, lam_init):
    qi = pl.program_id(1)
    h1 = h1_ref[0]
    _diff_project_q(h1, nb_ref, win_ref, h_scr, q_scr)
    lam = _diff_lambda(lp_ref, lam_init)
    diag = (lax.broadcasted_iota(jnp.int32, (SUB, SUB), 1) // CHUNK) <= (
        lax.broadcasted_iota(jnp.int32, (SUB, SUB), 0) // CHUNK)

    for case in range(NQ):
        @pl.when(qi == case)
        def _(case=case):
            for sub in range(TQ // SUB):
                rows = slice(sub * SUB, (sub + 1) * SUB)
                kt = case * TQ + (sub + 1) * SUB
                for hh in range(N_HEADS):
                    ps, ls = [], []
                    for c in range(2):
                        s = _dot_nt(q_scr[hh, rows, c * DQK:(c + 1) * DQK],
                                    kb_ref[0, hh, :kt, c * DQK:(c + 1) * DQK])
                        last = jnp.where(diag, s[:, kt - SUB:], NEG_BIG)
                        s = last if kt == SUB else jnp.concatenate([s[:, :kt - SUB], last], axis=1)
                        p = jnp.exp2(s - jnp.max(s, axis=-1, keepdims=True))
                        ls.append(jnp.sum(p, axis=-1, keepdims=True))
                        ps.append(p.astype(BF16))
                    pv = _dot(jnp.concatenate(ps, axis=0), vb_ref[0, hh, :kt, :])
                    o_scr[hh, rows] = pv[:SUB] / ls[0] - lam * (pv[SUB:] / ls[1])

    _diff_finish(h1, lambda hh: o_scr[hh], lam_init, h_scr, win_ref, sub_ref, mk_ref, mv_ref, wout_ref, nf_ref,
                 cat_scr, y_ref)


def _layer_b_prompt(h1, kb, vb, mkb, mvb, norm_b, w_in, lam_p, subln, w_out, norm_f, *, TQ, SUB, lam_init,
                    mem_layer):
    B, T, D = h1.shape
    kv_spec = pl.BlockSpec((1, N_HEADS, T, HEAD_W), lambda b, q: (b, 0, 0, 0))
    f = pl.pallas_call(
        functools.partial(_layer_b_prompt_kernel, TQ=TQ, SUB=SUB, NQ=T // TQ, lam_init=lam_init),
        grid=(B, T // TQ),
        in_specs=[pl.BlockSpec((1, TQ, D), lambda b, q: (b, q, 0)), kv_spec, kv_spec,
                  _mem_spec(mkb, mem_layer, 1), _mem_spec(mvb, mem_layer, 1),
                  _const_spec((1, D)), _const_spec(w_in.shape), _const_spec(lam_p.shape),
                  _const_spec((1, HEAD_W)), _const_spec(w_out.shape), _const_spec((1, D))],
        out_specs=pl.BlockSpec((1, TQ, D), lambda b, q: (b, q, 0)),
        out_shape=jax.ShapeDtypeStruct((B, T, D), F32),
        scratch_shapes=[pltpu.VMEM((TQ, D), BF16), pltpu.VMEM((N_HEADS, TQ, HEAD_W), BF16),
                        pltpu.VMEM((N_HEADS, TQ, HEAD_W), F32),
                        pltpu.VMEM((TQ, N_HEADS * HEAD_W + MEM_W), BF16)],
        compiler_params=pltpu.CompilerParams(
            dimension_semantics=("arbitrary", "arbitrary"), vmem_limit_bytes=VMEM_LIMIT_BYTES),
        name="layer_b_prompt",
    )
    return f(h1, kb, vb, mkb, mvb, norm_b.reshape(1, D), w_in, lam_p, subln.reshape(1, HEAD_W), w_out,
             norm_f.reshape(1, D))


def _online_tile(q_scr, k_of, v_of, m_scr, l_scr, acc_scr, rows):
    TQ = rows.stop - rows.start
    for hh in range(N_HEADS):
        ps, alphas = [], []
        for c in range(2):
            i = 2 * hh + c
            s = _dot_nt(q_scr[hh, rows, c * DQK:(c + 1) * DQK], k_of(hh, c))
            m_prev = m_scr[i, rows]
            m_new = jnp.maximum(m_prev, jnp.max(s, axis=-1, keepdims=True))
            alpha = jnp.exp2(m_prev - m_new)
            p = jnp.exp2(s - m_new)
            l_scr[i, rows] = alpha * l_scr[i, rows] + jnp.sum(p, axis=-1, keepdims=True)
            m_scr[i, rows] = m_new
            ps.append(p.astype(BF16))
            alphas.append(alpha)
        pv = _dot(jnp.concatenate(ps, axis=0), v_of(hh))
        acc_scr[2 * hh, rows] = alphas[0] * acc_scr[2 * hh, rows] + pv[:TQ]
        acc_scr[2 * hh + 1, rows] = alphas[1] * acc_scr[2 * hh + 1, rows] + pv[TQ:]


def _layer_b_sample_kernel(h1_ref, ck_ref, cv_ref, kn_ref, vn_ref, mk_ref, mv_ref, nb_ref, win_ref, lp_ref, sub_ref,
                           wout_ref, nf_ref, y_ref, h_scr, q_scr, m_scr, l_scr, acc_scr, cat_scr, *, TK, lam_init):
    kt = pl.program_id(1)
    NB, T, D = h1_ref.shape
    slot_rows = [slice(nb * T, (nb + 1) * T) for nb in range(NB)]

    @pl.when(kt == 0)
    def _():
        _diff_project_q(h1_ref[...].reshape(NB * T, D), nb_ref, win_ref, h_scr, q_scr)
        m_scr[...] = jnp.full(m_scr.shape, NEG_BIG, F32)
        l_scr[...] = jnp.zeros(l_scr.shape, F32)
        acc_scr[...] = jnp.zeros(acc_scr.shape, F32)

    for nb in range(NB):
        def ck_of(hh, c, nb=nb):
            return ck_ref[nb, pl.ds(2 * hh + c, TK, stride=KV_GROUPS), :].astype(BF16)

        def cv_of(hh, nb=nb):
            return jnp.concatenate([cv_ref[nb, pl.ds(_v_row(hh, dt), TK, stride=KV_GROUPS), :] for dt in range(2)],
                                   axis=1).astype(BF16)

        _online_tile(q_scr, ck_of, cv_of, m_scr, l_scr, acc_scr, slot_rows[nb])

    @pl.when(kt == pl.num_programs(1) - 1)
    def _():
        for nb in range(NB):
            _online_tile(q_scr,
                         lambda hh, c, nb=nb: kn_ref[nb, hh, :, c * DQK:(c + 1) * DQK],
                         lambda hh, nb=nb: vn_ref[nb, hh],
                         m_scr, l_scr, acc_scr, slot_rows[nb])
        lam = _diff_lambda(lp_ref, lam_init)
        o_of = lambda hh: (acc_scr[2 * hh] / l_scr[2 * hh]
                           - lam * (acc_scr[2 * hh + 1] / l_scr[2 * hh + 1]))
        _diff_finish(h1_ref[...].reshape(NB * T, D), o_of, lam_init, h_scr, win_ref, sub_ref, mk_ref, mv_ref,
                     wout_ref, nf_ref, cat_scr, y_ref, NB=NB)


def _layer_b_sample(h1, ck_rows, cv_rows, kb, vb, mkb, mvb, norm_b, w_in, lam_p, subln, w_out, norm_f, *, NB, TK,
                    lam_init, mem_layer):
    B, T, D = h1.shape
    R = NB * T
    nkt = ck_rows.shape[1] // (TK * KV_GROUPS)
    b0 = lambda b, k: (b, 0, 0)
    cache_spec = pl.BlockSpec((NB, TK * KV_GROUPS, LANES), lambda b, k: (b, k, 0))
    new_spec = pl.BlockSpec((NB, N_HEADS, T, HEAD_W), lambda b, k: (b, 0, 0, 0))
    f = pl.pallas_call(
        functools.partial(_layer_b_sample_kernel, TK=TK, lam_init=lam_init),
        grid=(B // NB, nkt),
        in_specs=[pl.BlockSpec((NB, T, D), b0), cache_spec, cache_spec, new_spec, new_spec,
                  _mem_spec(mkb, mem_layer, NB), _mem_spec(mvb, mem_layer, NB),
                  _const_spec((1, D)), _const_spec(w_in.shape), _const_spec(lam_p.shape),
                  _const_spec((1, HEAD_W)), _const_spec(w_out.shape), _const_spec((1, D))],
        out_specs=pl.BlockSpec((NB, T, D), b0),
        out_shape=jax.ShapeDtypeStruct((B, T, D), F32),
        scratch_shapes=[pltpu.VMEM((R, D), BF16), pltpu.VMEM((N_HEADS, R, HEAD_W), BF16),
                        pltpu.VMEM((2 * N_HEADS, R, 1), F32), pltpu.VMEM((2 * N_HEADS, R, 1), F32),
                        pltpu.VMEM((2 * N_HEADS, R, HEAD_W), F32),
                        pltpu.VMEM((R, N_HEADS * HEAD_W + MEM_W), BF16)],
        compiler_params=pltpu.CompilerParams(
            dimension_semantics=("arbitrary", "arbitrary"), vmem_limit_bytes=VMEM_LIMIT_BYTES),
        name="layer_b_sample",
    )
    return f(h1, ck_rows, cv_rows, kb, vb, mkb, mvb, norm_b.reshape(1, D), w_in, lam_p, subln.reshape(1, HEAD_W),
             w_out, norm_f.reshape(1, D))


def _rope_tables(pos):
    half = DQK
    inv = ROPE_BASE ** (-jnp.arange(half, dtype=F32) / half)
    ang = pos.astype(F32)[:, None] * inv[None, :]
    return jnp.cos(ang), jnp.sin(ang)


def _pick(n, candidates):
    for c in candidates:
        if n % c == 0:
            return c
    return n


def kernel(x_prompt, x_sample, state_ret, cache_k, cache_v, cache_mem_k, cache_mem_v, mem_prompt,
           norm_a, w_in_a, w_out_a, norm_kv, w_kv, norm_b, w_in_b, diff_lambda, subln_b, w_out_b,
           norm_mem, w_mem_kv, norm_f):
    depth = norm_mem.shape[0]
    assert norm_a.shape[0] == 1 and norm_b.shape[0] == 1 and depth == 2, "kernel is written for depth 2"
    lam_init = 0.8 - 0.6 * math.exp(-0.3 * 1)
    wia, woa, wkv = w_in_a[0].astype(BF16), w_out_a[0].astype(BF16), w_kv.astype(BF16)
    wib, wob = w_in_b[0].astype(BF16), w_out_b[0].astype(BF16)

    Bp, Tp, D = x_prompt.shape
    mk_p, mv_p, mkb_p, mvb_p = _memory_kv(mem_prompt, norm_mem, w_mem_kv)
    cos_p, sin_p = _rope_tables(jnp.arange(Tp))
    blk = _pick(Tp, (256, 128, CHUNK))
    tt = _pick(Tp, (2 * blk, blk))
    h1_p, k_p, v_p, kb_p, vb_p, S_p = _layer_a(
        x_prompt, cos_p, sin_p, None, mkb_p, mvb_p, norm_a[0], wia, woa, norm_kv, wkv, NB=1, TT=tt, L=blk,
        mem_layer=0)
    y_p = _layer_b_prompt(h1_p, kb_p, vb_p, mkb_p, mvb_p, norm_b[0], wib, diff_lambda[0], subln_b[0], wob,
                          norm_f, TQ=blk, SUB=blk, lam_init=lam_init, mem_layer=1)

    Bs, Ts, _ = x_sample.shape
    past = cache_k.shape[1]
    nbs = _pick(Bs, (4, 2, 1))
    cos_s, sin_s = _rope_tables(past + jnp.arange(Ts))
    cos_s, sin_s = jnp.tile(cos_s, (nbs, 1)), jnp.tile(sin_s, (nbs, 1))
    M = cache_mem_k.shape[2]
    cmk = cache_mem_k.reshape(depth, Bs, M * N_HEADS, MEM_HD)
    cmv = cache_mem_v.reshape(depth, Bs, M * N_HEADS, MEM_HD)
    h1_s, k_s, v_s, kb_s, vb_s, S_s = _layer_a(
        x_sample, cos_s, sin_s, state_ret[0], cmk, cmv, norm_a[0], wia, woa, norm_kv, wkv,
        NB=nbs, TT=Ts, L=Ts, mem_layer=0)
    ck_rows = cache_k.reshape(Bs, past * KV_GROUPS, LANES)
    cv_rows = cache_v.reshape(Bs, past, N_HEADS, 2, LANES).transpose(0, 1, 3, 2, 4).reshape(
        Bs, past * KV_GROUPS, LANES)
    nbb = _pick(Bs, (2, 1))
    y_s = _layer_b_sample(h1_s, ck_rows, cv_rows, kb_s, vb_s, cmk, cmv, norm_b[0], wib, diff_lambda[0],
                          subln_b[0], wob, norm_f, NB=nbb, TK=_pick(past, (2048 // nbb, 256, 128)), lam_init=lam_init,
                          mem_layer=1)

    return (y_p, y_s, S_p[None], _k_view_to_5d(k_p, Bp, Tp), _v_view_to_4d(v_p, Bp, Tp),
            mk_p, mv_p, S_s[None], _k_view_to_5d(k_s, Bs, Ts), _v_view_to_4d(v_s, Bs, Ts))
```

```python
import functools
import math

import jax
import jax.numpy as jnp
from jax import lax
from jax.experimental import pallas as pl
from jax.experimental.pallas import tpu as pltpu

F32 = jnp.float32
BF16 = jnp.bfloat16

EPS = 1e-6
ROPE_BASE = 10000.0
CHUNK = 64
N_HEADS = 4
HEAD_W = 256
DQK = 128
MEM_HD = 128
MEM_W = N_HEADS * MEM_HD
LANES = 128
KV_GROUPS = N_HEADS * HEAD_W // LANES
NEG_BIG = -1e30
LOG2E = math.log2(math.e)

VMEM_LIMIT_BYTES = 56 * 1024 * 1024


def _dot(a, b):
    return jnp.dot(a, b, preferred_element_type=F32)


def _dot_nt(a, b):
    return lax.dot_general(a, b, (((1,), (1,)), ((), ())), preferred_element_type=F32)


def _dot_tn(a, b):
    return lax.dot_general(a, b, (((0,), (0,)), ((), ())), preferred_element_type=F32)


def _rms(x):
    return x * lax.rsqrt(jnp.mean(x * x, axis=-1, keepdims=True) + EPS)


def _silu(g):
    return (0.5 * g) * (1.0 + jnp.tanh(0.5 * g))


def _mem_spec(arr, layer, nb):
    zeros = (0,) * (arr.ndim - 2)
    return pl.BlockSpec((None, nb) + arr.shape[2:], lambda b, t: (layer, b) + zeros)


def _const_spec(shape):
    nd = len(shape)
    return pl.BlockSpec(shape, lambda *_: (0,) * nd, pipeline_mode=pl.Buffered(1))


def _v_row(h, dt):
    return dt * N_HEADS + h


def _mem_attention(qm, gm, mk_ref, mv_ref, nb, rows, cat_scr, col0):
    row_view = len(mk_ref.shape) == 3
    for hh in range(N_HEADS):
        cs = slice(hh * MEM_HD, (hh + 1) * MEM_HD)
        if row_view:
            hsel = pl.ds(hh, mk_ref.shape[1] // N_HEADS, stride=N_HEADS)
            mk, mv = mk_ref[nb, hsel, :].astype(BF16), mv_ref[nb, hsel, :].astype(BF16)
        else:
            mk, mv = mk_ref[nb, hh], mv_ref[nb, hh]
        q = (qm[rows, cs] * (MEM_HD ** -0.5 * LOG2E)).astype(BF16)
        s = _dot_nt(q, mk)
        p = jnp.exp2(s - jnp.max(s, axis=-1, keepdims=True))
        l = jnp.sum(p, axis=-1, keepdims=True)
        o = _dot(p.astype(BF16), mv) / l
        cat_scr[rows, col0 + hh * MEM_HD: col0 + (hh + 1) * MEM_HD] = (o * _silu(gm[rows, cs])).astype(BF16)


def _memkv_kernel(mem_ref, g_ref, w_ref, mk_ref, mv_ref, mkb_ref, mvb_ref):
    tm = mem_ref.shape[0]
    base = _rms(mem_ref[...])
    nbm, M = mkb_ref.shape[1], mkb_ref.shape[3]
    for l in range(w_ref.shape[0]):
        kv = _dot((base * g_ref[l]).astype(BF16), w_ref[l])
        mk, mv = kv[:, :MEM_W], kv[:, MEM_W:]
        for hh in range(N_HEADS):
            cs = slice(hh * MEM_HD, (hh + 1) * MEM_HD)
            mk_ref[l, pl.ds(hh, tm, stride=N_HEADS), :] = mk[:, cs]
            mv_ref[l, pl.ds(hh, tm, stride=N_HEADS), :] = mv[:, cs]
            for bb in range(nbm):
                mkb_ref[l, bb, hh] = mk[bb * M:(bb + 1) * M, cs].astype(BF16)
                mvb_ref[l, bb, hh] = mv[bb * M:(bb + 1) * M, cs].astype(BF16)


def _memory_kv(mem, norm_mem, w_mem_kv):
    B, M, D = mem.shape
    nl = norm_mem.shape[0]
    rows = B * M
    tm = 512 if rows % 512 == 0 else M
    f = pl.pallas_call(
        _memkv_kernel,
        grid=(rows // tm,),
        in_specs=[
            pl.BlockSpec((tm, D), lambda i: (i, 0)),
            _const_spec((nl, 1, D)),
            _const_spec((nl, D, 2 * MEM_W)),
        ],
        out_specs=[pl.BlockSpec((nl, tm * N_HEADS, MEM_HD), lambda i: (0, i, 0))] * 2
        + [pl.BlockSpec((nl, tm // M, N_HEADS, M, MEM_HD), lambda i: (0, i, 0, 0, 0))] * 2,
        out_shape=[jax.ShapeDtypeStruct((nl, rows * N_HEADS, MEM_HD), F32)] * 2
        + [jax.ShapeDtypeStruct((nl, B, N_HEADS, M, MEM_HD), BF16)] * 2,
        compiler_params=pltpu.CompilerParams(
            dimension_semantics=("arbitrary",), vmem_limit_bytes=VMEM_LIMIT_BYTES),
        name="memkv",
    )
    mk, mv, mkb, mvb = f(mem.reshape(rows, D), norm_mem.reshape(nl, 1, D), w_mem_kv.astype(BF16))
    shp5 = (nl, B, M, N_HEADS, MEM_HD)
    return mk.reshape(shp5), mv.reshape(shp5), mkb, mvb


def _layer_a_kernel(*refs, NB, TT, L, has_state):
    it = iter(refs)
    x_ref, cos_ref, sin_ref = next(it), next(it), next(it)
    s0_ref = next(it) if has_state else None
    mk_ref, mv_ref, na_ref, win_ref, wout_ref, nkv_ref, wkv_ref = (next(it) for _ in range(7))
    h1_ref, k_ref, v_ref, kb_ref, vb_ref, S_ref = (next(it) for _ in range(6))
    h_scr, cat_scr, dec_scr = (next(it) for _ in range(3))

    R = NB * TT
    D = x_ref.shape[-1]
    QW = N_HEADS * HEAD_W
    t = pl.program_id(1)
    log_g = [math.log(1.0 - 2.0 ** (-5.0 - h)) for h in range(N_HEADS)]

    @pl.when((pl.program_id(0) == 0) & (t == 0))
    def _():
        d = (lax.broadcasted_iota(jnp.int32, (L, L), 0) - lax.broadcasted_iota(jnp.int32, (L, L), 1)).astype(F32)
        for h in range(N_HEADS):
            dec_scr[h] = jnp.where(d >= 0, jnp.exp(log_g[h] * jnp.maximum(d, 0.0)), 0.0)

    @pl.when(t == 0)
    def _():
        if has_state:
            S_ref[...] = s0_ref[...]
        else:
            S_ref[...] = jnp.zeros(S_ref.shape, F32)

    x = x_ref[...].reshape(R, D)
    h_scr[...] = (_rms(x) * na_ref[...]).astype(BF16)
    cos = cos_ref[...]
    sin = sin_ref[...]
    half = HEAD_W // 2
    idx = lax.broadcasted_iota(jnp.int32, (L, 1), 0).astype(F32)

    def rope(c0):
        y = _dot(h_scr[...], win_ref[:, c0:c0 + HEAD_W])
        y1, y2 = y[:, :half], y[:, half:]
        return jnp.concatenate([y1 * cos - y2 * sin, y2 * cos + y1 * sin], axis=-1)

    for h in range(N_HEADS):
        q = rope(h * HEAD_W).astype(BF16)
        k = rope(QW + h * HEAD_W) * (HEAD_W ** -0.5)
        v = _dot(h_scr[...], win_ref[:, 2 * QW + h * HEAD_W: 2 * QW + (h + 1) * HEAD_W]).astype(BF16)
        g = _dot(h_scr[...], win_ref[:, 3 * QW + h * HEAD_W: 3 * QW + (h + 1) * HEAD_W])
        a_in = jnp.exp(log_g[h] * (idx + 1.0))
        b_out = jnp.exp(log_g[h] * (L - 1.0 - idx))
        g_blk = math.exp(log_g[h] * L)
        for nb in range(NB):
            for j in range(TT // L):
                rows = slice(nb * TT + j * L, nb * TT + (j + 1) * L)
                qb, vb = q[rows], v[rows]
                kf = k[rows]
                S = S_ref[nb, h]
                sc = _dot_nt(qb, kf.astype(BF16)) * dec_scr[h]
                o = _dot(sc.astype(BF16), vb) + _dot(qb, S.astype(BF16)) * a_in
                S_ref[nb, h] = g_blk * S + _dot_tn((kf * b_out).astype(BF16), vb)
                cat_scr[rows, h * HEAD_W:(h + 1) * HEAD_W] = (_rms(o) * _silu(g[rows])).astype(BF16)

    qm = _dot(h_scr[...], win_ref[:, 4 * QW: 4 * QW + MEM_W])
    gm = _dot(h_scr[...], win_ref[:, 4 * QW + MEM_W: 4 * QW + 2 * MEM_W])
    for nb in range(NB):
        _mem_attention(qm, gm, mk_ref, mv_ref, nb, slice(nb * TT, (nb + 1) * TT), cat_scr, QW)

    NC = 512
    for c in range(D // NC):
        cs = slice(c * NC, (c + 1) * NC)
        y = _dot(cat_scr[...], wout_ref[:, cs]) + x_ref[:, :, cs].reshape(R, NC)
        h1_ref[:, :, cs] = y.reshape(NB, TT, NC)

    hk = (_rms(h1_ref[...].reshape(R, D)) * nkv_ref[...]).astype(BF16)
    KW = N_HEADS * HEAD_W
    for c in range(2 * KW // NC):
        kv = _dot(hk, wkv_ref[:, c * NC:(c + 1) * NC])
        is_k = c < KW // NC
        col0 = c * NC if is_k else c * NC - KW
        for hl in range(NC // HEAD_W):
            (kb_ref if is_k else vb_ref)[:, col0 // HEAD_W + hl] = (
                kv[:, hl * HEAD_W:(hl + 1) * HEAD_W].astype(BF16).reshape(NB, TT, HEAD_W))
        for jj in range(NC // LANES):
            col = col0 + jj * LANES
            if is_k:
                row = col // LANES
                k_ref[pl.ds(row, R, stride=KV_GROUPS), :] = kv[:, jj * LANES:(jj + 1) * LANES]
            else:
                row = _v_row(col // HEAD_W, (col % HEAD_W) // LANES)
                v_ref[pl.ds(row, R, stride=KV_GROUPS), :] = kv[:, jj * LANES:(jj + 1) * LANES]


def _layer_a(x, cos, sin, s0, mkb, mvb, norm_a, w_in, w_out, norm_kv, w_kv, *, NB, TT, L, mem_layer):
    B, T, D = x.shape
    R = NB * TT
    nt = T // TT
    assert NB == 1 or nt == 1
    KW = w_kv.shape[1] // 2
    has_state = s0 is not None
    bt = lambda b, t: (b, t, 0)
    b0 = lambda b, t: (b, 0, 0)
    in_specs = [pl.BlockSpec((NB, TT, D), bt),
                pl.BlockSpec((R, DQK), lambda b, t: (t, 0)),
                pl.BlockSpec((R, DQK), lambda b, t: (t, 0))]
    args = [x, cos, sin]
    if has_state:
        in_specs.append(pl.BlockSpec((NB, N_HEADS, HEAD_W, HEAD_W), lambda b, t: (b, 0, 0, 0)))
        args.append(s0)
    in_specs += [_mem_spec(mkb, mem_layer, NB), _mem_spec(mvb, mem_layer, NB),
                 _const_spec((1, D)), _const_spec(w_in.shape), _const_spec(w_out.shape),
                 _const_spec((1, D)), _const_spec(w_kv.shape)]
    args += [mkb, mvb, norm_a.reshape(1, D), w_in, w_out, norm_kv.reshape(1, D), w_kv]
    rows_spec = pl.BlockSpec((R * KV_GROUPS, LANES), lambda b, t: (b * nt + t, 0))
    assert KW == N_HEADS * HEAD_W
    hm_spec = pl.BlockSpec((NB, N_HEADS, TT, HEAD_W), lambda b, t: (b, 0, t, 0))
    out_specs = [pl.BlockSpec((NB, TT, D), bt), rows_spec, rows_spec, hm_spec, hm_spec,
                 pl.BlockSpec((NB, N_HEADS, HEAD_W, HEAD_W), lambda b, t: (b, 0, 0, 0))]
    out_shape = [jax.ShapeDtypeStruct((B, T, D), F32)] + [
        jax.ShapeDtypeStruct((B * T * KV_GROUPS, LANES), F32)] * 2 + [
        jax.ShapeDtypeStruct((B, N_HEADS, T, HEAD_W), BF16)] * 2 + [
        jax.ShapeDtypeStruct((B, N_HEADS, HEAD_W, HEAD_W), F32)]
    f = pl.pallas_call(
        functools.partial(_layer_a_kernel, NB=NB, TT=TT, L=L, has_state=has_state),
        grid=(B // NB, nt),
        in_specs=in_specs,
        out_specs=out_specs,
        out_shape=out_shape,
        scratch_shapes=[pltpu.VMEM((R, D), BF16), pltpu.VMEM((R, N_HEADS * HEAD_W + MEM_W), BF16),
                        pltpu.VMEM((N_HEADS, L, L), F32)],
        compiler_params=pltpu.CompilerParams(
            dimension_semantics=("arbitrary", "arbitrary"), vmem_limit_bytes=VMEM_LIMIT_BYTES),
        name="layer_a",
    )
    return f(*args)


def _k_view_to_5d(kr, B, T):
    return kr.reshape(B, T, N_HEADS, 2, DQK)


def _v_view_to_4d(vr, B, T):
    return vr.reshape(B, T, 2, N_HEADS, LANES).transpose(0, 1, 3, 2, 4).reshape(B, T, N_HEADS, HEAD_W)


def _diff_lambda(lp_ref, lam_init):
    lp = lp_ref[...]
    a = jnp.sum(lp[0:1] * lp[1:2], axis=-1, keepdims=True)
    b = jnp.sum(lp[2:3] * lp[3:4], axis=-1, keepdims=True)
    return jnp.exp(a) - jnp.exp(b) + lam_init


def _diff_project_q(h1, nb_ref, win_ref, h_scr, q_scr):
    h_scr[...] = (_rms(h1) * nb_ref[...]).astype(BF16)
    for hh in range(N_HEADS):
        q = _dot(h_scr[...], win_ref[:, hh * HEAD_W:(hh + 1) * HEAD_W]) * (DQK ** -0.5 * LOG2E)
        q_scr[hh] = q.astype(BF16)


def _diff_finish(h1, o_of, lam_init, h_scr, win_ref, sub_ref, mk_ref, mv_ref, wout_ref, nf_ref, cat_scr, y_ref, NB=1):
    R, D = h1.shape
    T = R // NB
    QW = N_HEADS * HEAD_W
    for hh in range(N_HEADS):
        o = _rms(o_of(hh)) * sub_ref[...] * (1.0 - lam_init)
        g = _dot(h_scr[...], win_ref[:, QW + hh * HEAD_W: QW + (hh + 1) * HEAD_W])
        cat_scr[:, hh * HEAD_W:(hh + 1) * HEAD_W] = (o * _silu(g)).astype(BF16)
    qm = _dot(h_scr[...], win_ref[:, 2 * QW: 2 * QW + MEM_W])
    gm = _dot(h_scr[...], win_ref[:, 2 * QW + MEM_W: 2 * QW + 2 * MEM_W])
    for nb in range(NB):
        _mem_attention(qm, gm, mk_ref, mv_ref, nb, slice(nb * T, (nb + 1) * T), cat_scr, QW)
    NC = 512
    for c in range(D // NC):
        cs = slice(c * NC, (c + 1) * NC)
        y_ref[:, :, cs] = (_dot(cat_scr[...], wout_ref[:, cs]) + h1[:, cs]).reshape(NB, T, NC)
    y_ref[...] = (_rms(y_ref[...].reshape(R, D)) * nf_ref[...]).reshape(NB, T, D)


def _layer_b_prompt_kernel(h1_ref, kb_ref, vb_ref, mk_ref, mv_ref, nb_ref, win_ref, lp_ref, sub_ref, wout_ref,
                           nf_ref, y_ref, h_scr, q_scr, o_scr, cat_scr, *, TQ, SUB, NQ, lam_init):
    qi = pl.program_id(1)
    h1 = h1_ref[0]
    _diff_project_q(h1, nb_ref, win_ref, h_scr, q_scr)
    lam = _diff_lambda(lp_ref, lam_init)
    diag = (lax.broadcasted_iota(jnp.int32, (SUB, SUB), 1) // CHUNK) <= (
        lax.broadcasted_iota(jnp.int32, (SUB, SUB), 0) // CHUNK)

    for case in range(NQ):
        @pl.when(qi == case)
        def _(case=case):
            for sub in range(TQ // SUB):
                rows = slice(sub * SUB, (sub + 1) * SUB)
                kt = case * TQ + (sub + 1) * SUB
                for hh in range(N_HEADS):
                    ps, ls = [], []
                    qh = q_scr[hh, rows, :]
                    zq = jnp.zeros((SUB, DQK), BF16)
                    qs = jnp.concatenate([jnp.concatenate([qh[:, :DQK], zq], axis=1),
                                          jnp.concatenate([zq, qh[:, DQK:]], axis=1)], axis=0)
                    s_all = _dot_nt(qs, kb_ref[0, hh, :kt, :])
                    for c in range(2):
                        s = s_all[c * SUB:(c + 1) * SUB]
                        last = jnp.where(diag, s[:, kt - SUB:], NEG_BIG)
                        s = last if kt == SUB else jnp.concatenate([s[:, :kt - SUB], last], axis=1)
                        p = jnp.exp2(s - jnp.max(s, axis=-1, keepdims=True))
                        ls.append(jnp.sum(p, axis=-1, keepdims=True))
                        ps.append(p.astype(BF16))
                    pv = _dot(jnp.concatenate(ps, axis=0), vb_ref[0, hh, :kt, :])
                    o_scr[hh, rows] = pv[:SUB] / ls[0] - lam * (pv[SUB:] / ls[1])

    _diff_finish(h1, lambda hh: o_scr[hh], lam_init, h_scr, win_ref, sub_ref, mk_ref, mv_ref, wout_ref, nf_ref,
                 cat_scr, y_ref)


def _layer_b_prompt(h1, kb, vb, mkb, mvb, norm_b, w_in, lam_p, subln, w_out, norm_f, *, TQ, SUB, lam_init,
                    mem_layer):
    B, T, D = h1.shape
    kv_spec = pl.BlockSpec((1, N_HEADS, T, HEAD_W), lambda b, q: (b, 0, 0, 0))
    f = pl.pallas_call(
        functools.partial(_layer_b_prompt_kernel, TQ=TQ, SUB=SUB, NQ=T // TQ, lam_init=lam_init),
        grid=(B, T // TQ),
        in_specs=[pl.BlockSpec((1, TQ, D), lambda b, q: (b, q, 0)), kv_spec, kv_spec,
                  _mem_spec(mkb, mem_layer, 1), _mem_spec(mvb, mem_layer, 1),
                  _const_spec((1, D)), _const_spec(w_in.shape), _const_spec(lam_p.shape),
                  _const_spec((1, HEAD_W)), _const_spec(w_out.shape), _const_spec((1, D))],
        out_specs=pl.BlockSpec((1, TQ, D), lambda b, q: (b, q, 0)),
        out_shape=jax.ShapeDtypeStruct((B, T, D), F32),
        scratch_shapes=[pltpu.VMEM((TQ, D), BF16), pltpu.VMEM((N_HEADS, TQ, HEAD_W), BF16),
                        pltpu.VMEM((N_HEADS, TQ, HEAD_W), F32),
                        pltpu.VMEM((TQ, N_HEADS * HEAD_W + MEM_W), BF16)],
        compiler_params=pltpu.CompilerParams(
            dimension_semantics=("arbitrary", "arbitrary"), vmem_limit_bytes=VMEM_LIMIT_BYTES),
        name="layer_b_prompt",
    )
    return f(h1, kb, vb, mkb, mvb, norm_b.reshape(1, D), w_in, lam_p, subln.reshape(1, HEAD_W), w_out,
             norm_f.reshape(1, D))


def _online_tile(q_scr, k_of, v_of, m_scr, l_scr, acc_scr, rows):
    TQ = rows.stop - rows.start
    for hh in range(N_HEADS):
        ps, alphas = [], []
        for c in range(2):
            i = 2 * hh + c
            s = _dot_nt(q_scr[hh, rows, c * DQK:(c + 1) * DQK], k_of(hh, c))
            m_prev = m_scr[i, rows]
            m_new = jnp.maximum(m_prev, jnp.max(s, axis=-1, keepdims=True))
            alpha = jnp.exp2(m_prev - m_new)
            p = jnp.exp2(s - m_new)
            l_scr[i, rows] = alpha * l_scr[i, rows] + jnp.sum(p, axis=-1, keepdims=True)
            m_scr[i, rows] = m_new
            ps.append(p.astype(BF16))
            alphas.append(alpha)
        pv = _dot(jnp.concatenate(ps, axis=0), v_of(hh))
        acc_scr[2 * hh, rows] = alphas[0] * acc_scr[2 * hh, rows] + pv[:TQ]
        acc_scr[2 * hh + 1, rows] = alphas[1] * acc_scr[2 * hh + 1, rows] + pv[TQ:]


def _layer_b_sample_kernel(h1_ref, ck_ref, cv_ref, kn_ref, vn_ref, mk_ref, mv_ref, nb_ref, win_ref, lp_ref, sub_ref,
                           wout_ref, nf_ref, y_ref, h_scr, q_scr, m_scr, l_scr, acc_scr, cat_scr, *, TK, lam_init):
    kt = pl.program_id(1)
    NB, T, D = h1_ref.shape
    slot_rows = [slice(nb * T, (nb + 1) * T) for nb in range(NB)]

    @pl.when(kt == 0)
    def _():
        _diff_project_q(h1_ref[...].reshape(NB * T, D), nb_ref, win_ref, h_scr, q_scr)
        m_scr[...] = jnp.full(m_scr.shape, NEG_BIG, F32)
        l_scr[...] = jnp.zeros(l_scr.shape, F32)
        acc_scr[...] = jnp.zeros(acc_scr.shape, F32)

    for nb in range(NB):
        def ck_of(hh, c, nb=nb):
            return ck_ref[nb, pl.ds(2 * hh + c, TK, stride=KV_GROUPS), :].astype(BF16)

        def cv_of(hh, nb=nb):
            return jnp.concatenate([cv_ref[nb, pl.ds(_v_row(hh, dt), TK, stride=KV_GROUPS), :] for dt in range(2)],
                                   axis=1).astype(BF16)

        _online_tile(q_scr, ck_of, cv_of, m_scr, l_scr, acc_scr, slot_rows[nb])

    @pl.when(kt == pl.num_programs(1) - 1)
    def _():
        for nb in range(NB):
            _online_tile(q_scr,
                         lambda hh, c, nb=nb: kn_ref[nb, hh, :, c * DQK:(c + 1) * DQK],
                         lambda hh, nb=nb: vn_ref[nb, hh],
                         m_scr, l_scr, acc_scr, slot_rows[nb])
        lam = _diff_lambda(lp_ref, lam_init)
        o_of = lambda hh: (acc_scr[2 * hh] / l_scr[2 * hh]
                           - lam * (acc_scr[2 * hh + 1] / l_scr[2 * hh + 1]))
        _diff_finish(h1_ref[...].reshape(NB * T, D), o_of, lam_init, h_scr, win_ref, sub_ref, mk_ref, mv_ref,
                     wout_ref, nf_ref, cat_scr, y_ref, NB=NB)


def _layer_b_sample(h1, ck_rows, cv_rows, kb, vb, mkb, mvb, norm_b, w_in, lam_p, subln, w_out, norm_f, *, NB, TK,
                    lam_init, mem_layer):
    B, T, D = h1.shape
    R = NB * T
    nkt = ck_rows.shape[1] // (TK * KV_GROUPS)
    b0 = lambda b, k: (b, 0, 0)
    cache_spec = pl.BlockSpec((NB, TK * KV_GROUPS, LANES), lambda b, k: (b, k, 0))
    new_spec = pl.BlockSpec((NB, N_HEADS, T, HEAD_W), lambda b, k: (b, 0, 0, 0))
    f = pl.pallas_call(
        functools.partial(_layer_b_sample_kernel, TK=TK, lam_init=lam_init),
        grid=(B // NB, nkt),
        in_specs=[pl.BlockSpec((NB, T, D), b0), cache_spec, cache_spec, new_spec, new_spec,
                  _mem_spec(mkb, mem_layer, NB), _mem_spec(mvb, mem_layer, NB),
                  _const_spec((1, D)), _const_spec(w_in.shape), _const_spec(lam_p.shape),
                  _const_spec((1, HEAD_W)), _const_spec(w_out.shape), _const_spec((1, D))],
        out_specs=pl.BlockSpec((NB, T, D), b0),
        out_shape=jax.ShapeDtypeStruct((B, T, D), F32),
        scratch_shapes=[pltpu.VMEM((R, D), BF16), pltpu.VMEM((N_HEADS, R, HEAD_W), BF16),
                        pltpu.VMEM((2 * N_HEADS, R, 1), F32), pltpu.VMEM((2 * N_HEADS, R, 1), F32),
                        pltpu.VMEM((2 * N_HEADS, R, HEAD_W), F32),
                        pltpu.VMEM((R, N_HEADS * HEAD_W + MEM_W), BF16)],
        compiler_params=pltpu.CompilerParams(
            dimension_semantics=("arbitrary", "arbitrary"), vmem_limit_bytes=VMEM_LIMIT_BYTES),
        name="layer_b_sample",
    )
    return f(h1, ck_rows, cv_rows, kb, vb, mkb, mvb, norm_b.reshape(1, D), w_in, lam_p, subln.reshape(1, HEAD_W),
             w_out, norm_f.reshape(1, D))


def _rope_tables(pos):
    half = DQK
    inv = ROPE_BASE ** (-jnp.arange(half, dtype=F32) / half)
    ang = pos.astype(F32)[:, None] * inv[None, :]
    return jnp.cos(ang), jnp.sin(ang)


def _pick(n, candidates):
    for c in candidates:
        if n % c == 0:
            return c
    return n


def kernel(x_prompt, x_sample, state_ret, cache_k, cache_v, cache_mem_k, cache_mem_v, mem_prompt,
           norm_a, w_in_a, w_out_a, norm_kv, w_kv, norm_b, w_in_b, diff_lambda, subln_b, w_out_b,
           norm_mem, w_mem_kv, norm_f):
    depth = norm_mem.shape[0]
    assert norm_a.shape[0] == 1 and norm_b.shape[0] == 1 and depth == 2, "kernel is written for depth 2"
    lam_init = 0.8 - 0.6 * math.exp(-0.3 * 1)
    wia, woa, wkv = w_in_a[0].astype(BF16), w_out_a[0].astype(BF16), w_kv.astype(BF16)
    wib, wob = w_in_b[0].astype(BF16), w_out_b[0].astype(BF16)

    Bp, Tp, D = x_prompt.shape
    mk_p, mv_p, mkb_p, mvb_p = _memory_kv(mem_prompt, norm_mem, w_mem_kv)
    cos_p, sin_p = _rope_tables(jnp.arange(Tp))
    blk = _pick(Tp, (256, 128, CHUNK))
    tt = _pick(Tp, (2 * blk, blk))
    h1_p, k_p, v_p, kb_p, vb_p, S_p = _layer_a(
        x_prompt, cos_p, sin_p, None, mkb_p, mvb_p, norm_a[0], wia, woa, norm_kv, wkv, NB=1, TT=tt, L=blk,
        mem_layer=0)
    y_p = _layer_b_prompt(h1_p, kb_p, vb_p, mkb_p, mvb_p, norm_b[0], wib, diff_lambda[0], subln_b[0], wob,
                          norm_f, TQ=blk, SUB=blk, lam_init=lam_init, mem_layer=1)

    Bs, Ts, _ = x_sample.shape
    past = cache_k.shape[1]
    nbs = _pick(Bs, (4, 2, 1))
    cos_s, sin_s = _rope_tables(past + jnp.arange(Ts))
    cos_s, sin_s = jnp.tile(cos_s, (nbs, 1)), jnp.tile(sin_s, (nbs, 1))
    M = cache_mem_k.shape[2]
    cmk = cache_mem_k.reshape(depth, Bs, M * N_HEADS, MEM_HD)
    cmv = cache_mem_v.reshape(depth, Bs, M * N_HEADS, MEM_HD)
    h1_s, k_s, v_s, kb_s, vb_s, S_s = _layer_a(
        x_sample, cos_s, sin_s, state_ret[0], cmk, cmv, norm_a[0], wia, woa, norm_kv, wkv,
        NB=nbs, TT=Ts, L=Ts, mem_layer=0)
    ck_rows = cache_k.reshape(Bs, past * KV_GROUPS, LANES)
    cv_rows = cache_v.reshape(Bs, past, N_HEADS, 2, LANES).transpose(0, 1, 3, 2, 4).reshape(
        Bs, past * KV_GROUPS, LANES)
    nbb = _pick(Bs, (2, 1))
    y_s = _layer_b_sample(h1_s, ck_rows, cv_rows, kb_s, vb_s, cmk, cmv, norm_b[0], wib, diff_lambda[0],
                          subln_b[0], wob, norm_f, NB=nbb, TK=_pick(past, (2048 // nbb, 256, 128)), lam_init=lam_init,
                          mem_layer=1)

    return (y_p, y_s, S_p[None], _k_view_to_5d(k_p, Bp, Tp), _v_view_to_4d(v_p, Bp, Tp),
            mk_p, mv_p, S_s[None], _k_view_to_5d(k_s, Bs, Ts), _v_view_to_4d(v_s, Bs, Ts))
```

```python
import functools
import math

import jax
import jax.numpy as jnp
from jax import lax
from jax.experimental import pallas as pl
from jax.experimental.pallas import tpu as pltpu

F32 = jnp.float32
BF16 = jnp.bfloat16

EPS = 1e-6
ROPE_BASE = 10000.0
CHUNK = 64
N_HEADS = 4
HEAD_W = 256
DQK = 128
MEM_HD = 128
MEM_W = N_HEADS * MEM_HD
LANES = 128
KV_GROUPS = N_HEADS * HEAD_W // LANES
NEG_BIG = -1e30
LOG2E = math.log2(math.e)

VMEM_LIMIT_BYTES = 56 * 1024 * 1024


def _dot(a, b):
    return jnp.dot(a, b, preferred_element_type=F32)


def _dot_nt(a, b):
    return lax.dot_general(a, b, (((1,), (1,)), ((), ())), preferred_element_type=F32)


def _dot_tn(a, b):
    return lax.dot_general(a, b, (((0,), (0,)), ((), ())), preferred_element_type=F32)


def _rms(x):
    return x * lax.rsqrt(jnp.mean(x * x, axis=-1, keepdims=True) + EPS)


def _silu(g):
    return (0.5 * g) * (1.0 + jnp.tanh(0.5 * g))


def _mem_spec(arr, layer, nb):
    zeros = (0,) * (arr.ndim - 2)
    return pl.BlockSpec((None, nb) + arr.shape[2:], lambda b, t: (layer, b) + zeros)


def _const_spec(shape):
    nd = len(shape)
    return pl.BlockSpec(shape, lambda *_: (0,) * nd, pipeline_mode=pl.Buffered(1))


def _v_row(h, dt):
    return dt * N_HEADS + h


def _mem_attention(qm, gm, mk_ref, mv_ref, nb, rows, cat_scr, col0):
    row_view = len(mk_ref.shape) == 3
    for hh in range(N_HEADS):
        cs = slice(hh * MEM_HD, (hh + 1) * MEM_HD)
        if row_view:
            hsel = pl.ds(hh, mk_ref.shape[1] // N_HEADS, stride=N_HEADS)
            mk, mv = mk_ref[nb, hsel, :].astype(BF16), mv_ref[nb, hsel, :].astype(BF16)
        else:
            mk, mv = mk_ref[nb, hh], mv_ref[nb, hh]
        q = (qm[rows, cs] * (MEM_HD ** -0.5 * LOG2E)).astype(BF16)
        s = _dot_nt(q, mk)
        p = jnp.exp2(s - jnp.max(s, axis=-1, keepdims=True))
        l = jnp.sum(p, axis=-1, keepdims=True)
        o = _dot(p.astype(BF16), mv) / l
        cat_scr[rows, col0 + hh * MEM_HD: col0 + (hh + 1) * MEM_HD] = (o * _silu(gm[rows, cs])).astype(BF16)


def _memkv_kernel(mem_ref, g_ref, w_ref, mk_ref, mv_ref, mkb_ref, mvb_ref):
    tm = mem_ref.shape[0]
    base = _rms(mem_ref[...])
    nbm, M = mkb_ref.shape[1], mkb_ref.shape[3]
    for l in range(w_ref.shape[0]):
        kv = _dot((base * g_ref[l]).astype(BF16), w_ref[l])
        mk, mv = kv[:, :MEM_W], kv[:, MEM_W:]
        for hh in range(N_HEADS):
            cs = slice(hh * MEM_HD, (hh + 1) * MEM_HD)
            mk_ref[l, pl.ds(hh, tm, stride=N_HEADS), :] = mk[:, cs]
            mv_ref[l, pl.ds(hh, tm, stride=N_HEADS), :] = mv[:, cs]
            for bb in range(nbm):
                mkb_ref[l, bb, hh] = mk[bb * M:(bb + 1) * M, cs].astype(BF16)
                mvb_ref[l, bb, hh] = mv[bb * M:(bb + 1) * M, cs].astype(BF16)


def _memory_kv(mem, norm_mem, w_mem_kv):
    B, M, D = mem.shape
    nl = norm_mem.shape[0]
    rows = B * M
    tm = 512 if rows % 512 == 0 else M
    f = pl.pallas_call(
        _memkv_kernel,
        grid=(rows // tm,),
        in_specs=[
            pl.BlockSpec((tm, D), lambda i: (i, 0)),
            _const_spec((nl, 1, D)),
            _const_spec((nl, D, 2 * MEM_W)),
        ],
        out_specs=[pl.BlockSpec((nl, tm * N_HEADS, MEM_HD), lambda i: (0, i, 0))] * 2
        + [pl.BlockSpec((nl, tm // M, N_HEADS, M, MEM_HD), lambda i: (0, i, 0, 0, 0))] * 2,
        out_shape=[jax.ShapeDtypeStruct((nl, rows * N_HEADS, MEM_HD), F32)] * 2
        + [jax.ShapeDtypeStruct((nl, B, N_HEADS, M, MEM_HD), BF16)] * 2,
        compiler_params=pltpu.CompilerParams(
            dimension_semantics=("arbitrary",), vmem_limit_bytes=VMEM_LIMIT_BYTES),
        name="memkv",
    )
    mk, mv, mkb, mvb = f(mem.reshape(rows, D), norm_mem.reshape(nl, 1, D), w_mem_kv.astype(BF16))
    shp5 = (nl, B, M, N_HEADS, MEM_HD)
    return mk.reshape(shp5), mv.reshape(shp5), mkb, mvb


def _layer_a_kernel(*refs, NB, TT, L, has_state):
    it = iter(refs)
    x_ref, cos_ref, sin_ref = next(it), next(it), next(it)
    s0_ref = next(it) if has_state else None
    mk_ref, mv_ref, na_ref, win_ref, wout_ref, nkv_ref, wkv_ref = (next(it) for _ in range(7))
    h1_ref, k_ref, v_ref, kb_ref, vb_ref, S_ref = (next(it) for _ in range(6))
    h_scr, cat_scr, dec_scr = (next(it) for _ in range(3))

    R = NB * TT
    D = x_ref.shape[-1]
    QW = N_HEADS * HEAD_W
    t = pl.program_id(1)
    log_g = [math.log(1.0 - 2.0 ** (-5.0 - h)) for h in range(N_HEADS)]

    @pl.when((pl.program_id(0) == 0) & (t == 0))
    def _():
        d = (lax.broadcasted_iota(jnp.int32, (L, L), 0) - lax.broadcasted_iota(jnp.int32, (L, L), 1)).astype(F32)
        for h in range(N_HEADS):
            dec_scr[h] = jnp.where(d >= 0, jnp.exp(log_g[h] * jnp.maximum(d, 0.0)), 0.0)

    @pl.when(t == 0)
    def _():
        if has_state:
            S_ref[...] = s0_ref[...]
        else:
            S_ref[...] = jnp.zeros(S_ref.shape, F32)

    x = x_ref[...].reshape(R, D)
    h_scr[...] = (_rms(x) * na_ref[...]).astype(BF16)
    cos = cos_ref[...]
    sin = sin_ref[...]
    half = HEAD_W // 2
    idx = lax.broadcasted_iota(jnp.int32, (L, 1), 0).astype(F32)

    def rope(c0):
        y = _dot(h_scr[...], win_ref[:, c0:c0 + HEAD_W])
        y1, y2 = y[:, :half], y[:, half:]
        return jnp.concatenate([y1 * cos - y2 * sin, y2 * cos + y1 * sin], axis=-1)

    for h in range(N_HEADS):
        qf = rope(h * HEAD_W)
        q = qf.astype(BF16)
        k = rope(QW + h * HEAD_W) * (HEAD_W ** -0.5)
        v = _dot(h_scr[...], win_ref[:, 2 * QW + h * HEAD_W: 2 * QW + (h + 1) * HEAD_W]).astype(BF16)
        g = _dot(h_scr[...], win_ref[:, 3 * QW + h * HEAD_W: 3 * QW + (h + 1) * HEAD_W])
        a_in = jnp.exp(log_g[h] * (idx + 1.0))
        b_out = jnp.exp(log_g[h] * (L - 1.0 - idx))
        g_blk = math.exp(log_g[h] * L)
        for nb in range(NB):
            for j in range(TT // L):
                rows = slice(nb * TT + j * L, nb * TT + (j + 1) * L)
                qb, vb = q[rows], v[rows]
                kf = k[rows]
                S = S_ref[nb, h]
                sc = _dot_nt(qb, kf.astype(BF16)) * dec_scr[h]
                if L % LANES == 0:
                    lhs = jnp.concatenate([sc.astype(BF16), (qf[rows] * a_in).astype(BF16)], axis=1)
                    o = _dot(lhs, jnp.concatenate([vb, S.astype(BF16)], axis=0))
                else:
                    o = _dot(sc.astype(BF16), vb) + _dot(qb, S.astype(BF16)) * a_in
                S_ref[nb, h] = g_blk * S + _dot_tn((kf * b_out).astype(BF16), vb)
                cat_scr[rows, h * HEAD_W:(h + 1) * HEAD_W] = (_rms(o) * _silu(g[rows])).astype(BF16)

    qm = _dot(h_scr[...], win_ref[:, 4 * QW: 4 * QW + MEM_W])
    gm = _dot(h_scr[...], win_ref[:, 4 * QW + MEM_W: 4 * QW + 2 * MEM_W])
    for nb in range(NB):
        _mem_attention(qm, gm, mk_ref, mv_ref, nb, slice(nb * TT, (nb + 1) * TT), cat_scr, QW)

    NC = 512
    for c in range(D // NC):
        cs = slice(c * NC, (c + 1) * NC)
        y = _dot(cat_scr[...], wout_ref[:, cs]) + x_ref[:, :, cs].reshape(R, NC)
        h1_ref[:, :, cs] = y.reshape(NB, TT, NC)

    hk = (_rms(h1_ref[...].reshape(R, D)) * nkv_ref[...]).astype(BF16)
    KW = N_HEADS * HEAD_W
    for c in range(2 * KW // NC):
        kv = _dot(hk, wkv_ref[:, c * NC:(c + 1) * NC])
        is_k = c < KW // NC
        col0 = c * NC if is_k else c * NC - KW
        for hl in range(NC // HEAD_W):
            (kb_ref if is_k else vb_ref)[:, col0 // HEAD_W + hl] = (
                kv[:, hl * HEAD_W:(hl + 1) * HEAD_W].astype(BF16).reshape(NB, TT, HEAD_W))
        for jj in range(NC // LANES):
            col = col0 + jj * LANES
            if is_k:
                row = col // LANES
                k_ref[pl.ds(row, R, stride=KV_GROUPS), :] = kv[:, jj * LANES:(jj + 1) * LANES]
            else:
                row = _v_row(col // HEAD_W, (col % HEAD_W) // LANES)
                v_ref[pl.ds(row, R, stride=KV_GROUPS), :] = kv[:, jj * LANES:(jj + 1) * LANES]


def _layer_a(x, cos, sin, s0, mkb, mvb, norm_a, w_in, w_out, norm_kv, w_kv, *, NB, TT, L, mem_layer):
    B, T, D = x.shape
    R = NB * TT
    nt = T // TT
    assert NB == 1 or nt == 1
    KW = w_kv.shape[1] // 2
    has_state = s0 is not None
    bt = lambda b, t: (b, t, 0)
    b0 = lambda b, t: (b, 0, 0)
    in_specs = [pl.BlockSpec((NB, TT, D), bt),
                pl.BlockSpec((R, DQK), lambda b, t: (t, 0)),
                pl.BlockSpec((R, DQK), lambda b, t: (t, 0))]
    args = [x, cos, sin]
    if has_state:
        in_specs.append(pl.BlockSpec((NB, N_HEADS, HEAD_W, HEAD_W), lambda b, t: (b, 0, 0, 0)))
        args.append(s0)
    in_specs += [_mem_spec(mkb, mem_layer, NB), _mem_spec(mvb, mem_layer, NB),
                 _const_spec((1, D)), _const_spec(w_in.shape), _const_spec(w_out.shape),
                 _const_spec((1, D)), _const_spec(w_kv.shape)]
    args += [mkb, mvb, norm_a.reshape(1, D), w_in, w_out, norm_kv.reshape(1, D), w_kv]
    rows_spec = pl.BlockSpec((R * KV_GROUPS, LANES), lambda b, t: (b * nt + t, 0))
    assert KW == N_HEADS * HEAD_W
    hm_spec = pl.BlockSpec((NB, N_HEADS, TT, HEAD_W), lambda b, t: (b, 0, t, 0))
    out_specs = [pl.BlockSpec((NB, TT, D), bt), rows_spec, rows_spec, hm_spec, hm_spec,
                 pl.BlockSpec((NB, N_HEADS, HEAD_W, HEAD_W), lambda b, t: (b, 0, 0, 0))]
    out_shape = [jax.ShapeDtypeStruct((B, T, D), F32)] + [
        jax.ShapeDtypeStruct((B * T * KV_GROUPS, LANES), F32)] * 2 + [
        jax.ShapeDtypeStruct((B, N_HEADS, T, HEAD_W), BF16)] * 2 + [
        jax.ShapeDtypeStruct((B, N_HEADS, HEAD_W, HEAD_W), F32)]
    f = pl.pallas_call(
        functools.partial(_layer_a_kernel, NB=NB, TT=TT, L=L, has_state=has_state),
        grid=(B // NB, nt),
        in_specs=in_specs,
        out_specs=out_specs,
        out_shape=out_shape,
        scratch_shapes=[pltpu.VMEM((R, D), BF16), pltpu.VMEM((R, N_HEADS * HEAD_W + MEM_W), BF16),
                        pltpu.VMEM((N_HEADS, L, L), F32)],
        compiler_params=pltpu.CompilerParams(
            dimension_semantics=("arbitrary", "arbitrary"), vmem_limit_bytes=VMEM_LIMIT_BYTES),
        name="layer_a",
    )
    return f(*args)


def _k_view_to_5d(kr, B, T):
    return kr.reshape(B, T, N_HEADS, 2, DQK)


def _v_view_to_4d(vr, B, T):
    return vr.reshape(B, T, 2, N_HEADS, LANES).transpose(0, 1, 3, 2, 4).reshape(B, T, N_HEADS, HEAD_W)


def _diff_lambda(lp_ref, lam_init):
    lp = lp_ref[...]
    a = jnp.sum(lp[0:1] * lp[1:2], axis=-1, keepdims=True)
    b = jnp.sum(lp[2:3] * lp[3:4], axis=-1, keepdims=True)
    return jnp.exp(a) - jnp.exp(b) + lam_init


def _diff_project_q(h1, nb_ref, win_ref, h_scr, q_scr):
    h_scr[...] = (_rms(h1) * nb_ref[...]).astype(BF16)
    for hh in range(N_HEADS):
        q = _dot(h_scr[...], win_ref[:, hh * HEAD_W:(hh + 1) * HEAD_W]) * (DQK ** -0.5 * LOG2E)
        q_scr[hh] = q.astype(BF16)


def _diff_finish(h1, o_of, lam_init, h_scr, win_ref, sub_ref, mk_ref, mv_ref, wout_ref, nf_ref, cat_scr, y_ref, NB=1):
    R, D = h1.shape
    T = R // NB
    QW = N_HEADS * HEAD_W
    for hh in range(N_HEADS):
        o = _rms(o_of(hh)) * sub_ref[...] * (1.0 - lam_init)
        g = _dot(h_scr[...], win_ref[:, QW + hh * HEAD_W: QW + (hh + 1) * HEAD_W])
        cat_scr[:, hh * HEAD_W:(hh + 1) * HEAD_W] = (o * _silu(g)).astype(BF16)
    qm = _dot(h_scr[...], win_ref[:, 2 * QW: 2 * QW + MEM_W])
    gm = _dot(h_scr[...], win_ref[:, 2 * QW + MEM_W: 2 * QW + 2 * MEM_W])
    for nb in range(NB):
        _mem_attention(qm, gm, mk_ref, mv_ref, nb, slice(nb * T, (nb + 1) * T), cat_scr, QW)
    NC = 512
    for c in range(D // NC):
        cs = slice(c * NC, (c + 1) * NC)
        y_ref[:, :, cs] = (_dot(cat_scr[...], wout_ref[:, cs]) + h1[:, cs]).reshape(NB, T, NC)
    y_ref[...] = (_rms(y_ref[...].reshape(R, D)) * nf_ref[...]).reshape(NB, T, D)


def _layer_b_prompt_kernel(h1_ref, kb_ref, vb_ref, mk_ref, mv_ref, nb_ref, win_ref, lp_ref, sub_ref, wout_ref,
                           nf_ref, y_ref, h_scr, q_scr, o_scr, cat_scr, *, TQ, SUB, NQ, lam_init):
    qi = pl.program_id(1)
    h1 = h1_ref[0]
    _diff_project_q(h1, nb_ref, win_ref, h_scr, q_scr)
    lam = _diff_lambda(lp_ref, lam_init)
    diag = (lax.broadcasted_iota(jnp.int32, (SUB, SUB), 1) // CHUNK) <= (
        lax.broadcasted_iota(jnp.int32, (SUB, SUB), 0) // CHUNK)

    for case in range(NQ):
        @pl.when(qi == case)
        def _(case=case):
            for sub in range(TQ // SUB):
                rows = slice(sub * SUB, (sub + 1) * SUB)
                kt = case * TQ + (sub + 1) * SUB
                for hh in range(N_HEADS):
                    ps, ls = [], []
                    qh = q_scr[hh, rows, :]
                    zq = jnp.zeros((SUB, DQK), BF16)
                    qs = jnp.concatenate([jnp.concatenate([qh[:, :DQK], zq], axis=1),
                                          jnp.concatenate([zq, qh[:, DQK:]], axis=1)], axis=0)
                    s_all = _dot_nt(qs, kb_ref[0, hh, :kt, :])
                    for c in range(2):
                        s = s_all[c * SUB:(c + 1) * SUB]
                        last = jnp.where(diag, s[:, kt - SUB:], NEG_BIG)
                        s = last if kt == SUB else jnp.concatenate([s[:, :kt - SUB], last], axis=1)
                        p = jnp.exp2(s - jnp.max(s, axis=-1, keepdims=True))
                        ls.append(jnp.sum(p, axis=-1, keepdims=True))
                        ps.append(p.astype(BF16))
                    pv = _dot(jnp.concatenate(ps, axis=0), vb_ref[0, hh, :kt, :])
                    o_scr[hh, rows] = pv[:SUB] / ls[0] - lam * (pv[SUB:] / ls[1])

    _diff_finish(h1, lambda hh: o_scr[hh], lam_init, h_scr, win_ref, sub_ref, mk_ref, mv_ref, wout_ref, nf_ref,
                 cat_scr, y_ref)


def _layer_b_prompt(h1, kb, vb, mkb, mvb, norm_b, w_in, lam_p, subln, w_out, norm_f, *, TQ, SUB, lam_init,
                    mem_layer):
    B, T, D = h1.shape
    kv_spec = pl.BlockSpec((1, N_HEADS, T, HEAD_W), lambda b, q: (b, 0, 0, 0))
    f = pl.pallas_call(
        functools.partial(_layer_b_prompt_kernel, TQ=TQ, SUB=SUB, NQ=T // TQ, lam_init=lam_init),
        grid=(B, T // TQ),
        in_specs=[pl.BlockSpec((1, TQ, D), lambda b, q: (b, q, 0)), kv_spec, kv_spec,
                  _mem_spec(mkb, mem_layer, 1), _mem_spec(mvb, mem_layer, 1),
                  _const_spec((1, D)), _const_spec(w_in.shape), _const_spec(lam_p.shape),
                  _const_spec((1, HEAD_W)), _const_spec(w_out.shape), _const_spec((1, D))],
        out_specs=pl.BlockSpec((1, TQ, D), lambda b, q: (b, q, 0)),
        out_shape=jax.ShapeDtypeStruct((B, T, D), F32),
        scratch_shapes=[pltpu.VMEM((TQ, D), BF16), pltpu.VMEM((N_HEADS, TQ, HEAD_W), BF16),
                        pltpu.VMEM((N_HEADS, TQ, HEAD_W), F32),
                        pltpu.VMEM((TQ, N_HEADS * HEAD_W + MEM_W), BF16)],
        compiler_params=pltpu.CompilerParams(
            dimension_semantics=("arbitrary", "arbitrary"), vmem_limit_bytes=VMEM_LIMIT_BYTES),
        name="layer_b_prompt",
    )
    return f(h1, kb, vb, mkb, mvb, norm_b.reshape(1, D), w_in, lam_p, subln.reshape(1, HEAD_W), w_out,
             norm_f.reshape(1, D))


def _online_tile(q_scr, k_of, v_of, m_scr, l_scr, acc_scr, rows):
    TQ = rows.stop - rows.start
    for hh in range(N_HEADS):
        ps, alphas = [], []
        for c in range(2):
            i = 2 * hh + c
            s = _dot_nt(q_scr[hh, rows, c * DQK:(c + 1) * DQK], k_of(hh, c))
            m_prev = m_scr[i, rows]
            m_new = jnp.maximum(m_prev, jnp.max(s, axis=-1, keepdims=True))
            alpha = jnp.exp2(m_prev - m_new)
            p = jnp.exp2(s - m_new)
            l_scr[i, rows] = alpha * l_scr[i, rows] + jnp.sum(p, axis=-1, keepdims=True)
            m_scr[i, rows] = m_new
            ps.append(p.astype(BF16))
            alphas.append(alpha)
        pv = _dot(jnp.concatenate(ps, axis=0), v_of(hh))
        acc_scr[2 * hh, rows] = alphas[0] * acc_scr[2 * hh, rows] + pv[:TQ]
        acc_scr[2 * hh + 1, rows] = alphas[1] * acc_scr[2 * hh + 1, rows] + pv[TQ:]


def _layer_b_sample_kernel(h1_ref, ck_ref, cv_ref, kn_ref, vn_ref, mk_ref, mv_ref, nb_ref, win_ref, lp_ref, sub_ref,
                           wout_ref, nf_ref, y_ref, h_scr, q_scr, m_scr, l_scr, acc_scr, cat_scr, *, TK, lam_init):
    kt = pl.program_id(1)
    NB, T, D = h1_ref.shape
    slot_rows = [slice(nb * T, (nb + 1) * T) for nb in range(NB)]

    @pl.when(kt == 0)
    def _():
        _diff_project_q(h1_ref[...].reshape(NB * T, D), nb_ref, win_ref, h_scr, q_scr)
        m_scr[...] = jnp.full(m_scr.shape, NEG_BIG, F32)
        l_scr[...] = jnp.zeros(l_scr.shape, F32)
        acc_scr[...] = jnp.zeros(acc_scr.shape, F32)

    for nb in range(NB):
        def ck_of(hh, c, nb=nb):
            return ck_ref[nb, pl.ds(2 * hh + c, TK, stride=KV_GROUPS), :].astype(BF16)

        def cv_of(hh, nb=nb):
            return jnp.concatenate([cv_ref[nb, pl.ds(_v_row(hh, dt), TK, stride=KV_GROUPS), :] for dt in range(2)],
                                   axis=1).astype(BF16)

        _online_tile(q_scr, ck_of, cv_of, m_scr, l_scr, acc_scr, slot_rows[nb])

    @pl.when(kt == pl.num_programs(1) - 1)
    def _():
        for nb in range(NB):
            _online_tile(q_scr,
                         lambda hh, c, nb=nb: kn_ref[nb, hh, :, c * DQK:(c + 1) * DQK],
                         lambda hh, nb=nb: vn_ref[nb, hh],
                         m_scr, l_scr, acc_scr, slot_rows[nb])
        lam = _diff_lambda(lp_ref, lam_init)
        o_of = lambda hh: (acc_scr[2 * hh] / l_scr[2 * hh]
                           - lam * (acc_scr[2 * hh + 1] / l_scr[2 * hh + 1]))
        _diff_finish(h1_ref[...].reshape(NB * T, D), o_of, lam_init, h_scr, win_ref, sub_ref, mk_ref, mv_ref,
                     wout_ref, nf_ref, cat_scr, y_ref, NB=NB)


def _layer_b_sample(h1, ck_rows, cv_rows, kb, vb, mkb, mvb, norm_b, w_in, lam_p, subln, w_out, norm_f, *, NB, TK,
                    lam_init, mem_layer):
    B, T, D = h1.shape
    R = NB * T
    nkt = ck_rows.shape[1] // (TK * KV_GROUPS)
    b0 = lambda b, k: (b, 0, 0)
    cache_spec = pl.BlockSpec((NB, TK * KV_GROUPS, LANES), lambda b, k: (b, k, 0))
    new_spec = pl.BlockSpec((NB, N_HEADS, T, HEAD_W), lambda b, k: (b, 0, 0, 0))
    f = pl.pallas_call(
        functools.partial(_layer_b_sample_kernel, TK=TK, lam_init=lam_init),
        grid=(B // NB, nkt),
        in_specs=[pl.BlockSpec((NB, T, D), b0), cache_spec, cache_spec, new_spec, new_spec,
                  _mem_spec(mkb, mem_layer, NB), _mem_spec(mvb, mem_layer, NB),
                  _const_spec((1, D)), _const_spec(w_in.shape), _const_spec(lam_p.shape),
                  _const_spec((1, HEAD_W)), _const_spec(w_out.shape), _const_spec((1, D))],
        out_specs=pl.BlockSpec((NB, T, D), b0),
        out_shape=jax.ShapeDtypeStruct((B, T, D), F32),
        scratch_shapes=[pltpu.VMEM((R, D), BF16), pltpu.VMEM((N_HEADS, R, HEAD_W), BF16),
                        pltpu.VMEM((2 * N_HEADS, R, 1), F32), pltpu.VMEM((2 * N_HEADS, R, 1), F32),
                        pltpu.VMEM((2 * N_HEADS, R, HEAD_W), F32),
                        pltpu.VMEM((R, N_HEADS * HEAD_W + MEM_W), BF16)],
        compiler_params=pltpu.CompilerParams(
            dimension_semantics=("arbitrary", "arbitrary"), vmem_limit_bytes=VMEM_LIMIT_BYTES),
        name="layer_b_sample",
    )
    return f(h1, ck_rows, cv_rows, kb, vb, mkb, mvb, norm_b.reshape(1, D), w_in, lam_p, subln.reshape(1, HEAD_W),
             w_out, norm_f.reshape(1, D))


def _rope_tables(pos):
    half = DQK
    inv = ROPE_BASE ** (-jnp.arange(half, dtype=F32) / half)
    ang = pos.astype(F32)[:, None] * inv[None, :]
    return jnp.cos(ang), jnp.sin(ang)


def _pick(n, candidates):
    for c in candidates:
        if n % c == 0:
            return c
    return n


def kernel(x_prompt, x_sample, state_ret, cache_k, cache_v, cache_mem_k, cache_mem_v, mem_prompt,
           norm_a, w_in_a, w_out_a, norm_kv, w_kv, norm_b, w_in_b, diff_lambda, subln_b, w_out_b,
           norm_mem, w_mem_kv, norm_f):
    depth = norm_mem.shape[0]
    assert norm_a.shape[0] == 1 and norm_b.shape[0] == 1 and depth == 2, "kernel is written for depth 2"
    lam_init = 0.8 - 0.6 * math.exp(-0.3 * 1)
    wia, woa, wkv = w_in_a[0].astype(BF16), w_out_a[0].astype(BF16), w_kv.astype(BF16)
    wib, wob = w_in_b[0].astype(BF16), w_out_b[0].astype(BF16)

    Bp, Tp, D = x_prompt.shape
    mk_p, mv_p, mkb_p, mvb_p = _memory_kv(mem_prompt, norm_mem, w_mem_kv)
    cos_p, sin_p = _rope_tables(jnp.arange(Tp))
    blk = _pick(Tp, (256, 128, CHUNK))
    tt = _pick(Tp, (2 * blk, blk))
    h1_p, k_p, v_p, kb_p, vb_p, S_p = _layer_a(
        x_prompt, cos_p, sin_p, None, mkb_p, mvb_p, norm_a[0], wia, woa, norm_kv, wkv, NB=1, TT=tt, L=blk,
        mem_layer=0)
    y_p = _layer_b_prompt(h1_p, kb_p, vb_p, mkb_p, mvb_p, norm_b[0], wib, diff_lambda[0], subln_b[0], wob,
                          norm_f, TQ=blk, SUB=blk, lam_init=lam_init, mem_layer=1)

    Bs, Ts, _ = x_sample.shape
    past = cache_k.shape[1]
    nbs = _pick(Bs, (4, 2, 1))
    cos_s, sin_s = _rope_tables(past + jnp.arange(Ts))
    cos_s, sin_s = jnp.tile(cos_s, (nbs, 1)), jnp.tile(sin_s, (nbs, 1))
    M = cache_mem_k.shape[2]
    cmk = cache_mem_k.reshape(depth, Bs, M * N_HEADS, MEM_HD)
    cmv = cache_mem_v.reshape(depth, Bs, M * N_HEADS, MEM_HD)
    h1_s, k_s, v_s, kb_s, vb_s, S_s = _layer_a(
        x_sample, cos_s, sin_s, state_ret[0], cmk, cmv, norm_a[0], wia, woa, norm_kv, wkv,
        NB=nbs, TT=Ts, L=Ts, mem_layer=0)
    ck_rows = cache_k.reshape(Bs, past * KV_GROUPS, LANES)
    cv_rows = cache_v.reshape(Bs, past, N_HEADS, 2, LANES).transpose(0, 1, 3, 2, 4).reshape(
        Bs, past * KV_GROUPS, LANES)
    nbb = _pick(Bs, (2, 1))
    y_s = _layer_b_sample(h1_s, ck_rows, cv_rows, kb_s, vb_s, cmk, cmv, norm_b[0], wib, diff_lambda[0],
                          subln_b[0], wob, norm_f, NB=nbb, TK=_pick(past, (2048 // nbb, 256, 128)), lam_init=lam_init,
                          mem_layer=1)

    return (y_p, y_s, S_p[None], _k_view_to_5d(k_p, Bp, Tp), _v_view_to_4d(v_p, Bp, Tp),
            mk_p, mv_p, S_s[None], _k_view_to_5d(k_s, Bs, Ts), _v_view_to_4d(v_s, Bs, Ts))
```

```python
import functools
import math

import jax
import jax.numpy as jnp
from jax import lax
from jax.experimental import pallas as pl
from jax.experimental.pallas import tpu as pltpu

F32 = jnp.float32
BF16 = jnp.bfloat16

EPS = 1e-6
ROPE_BASE = 10000.0
CHUNK = 64
N_HEADS = 4
HEAD_W = 256
DQK = 128
MEM_HD = 128
MEM_W = N_HEADS * MEM_HD
LANES = 128
KV_GROUPS = N_HEADS * HEAD_W // LANES
NEG_BIG = -1e30
LOG2E = math.log2(math.e)

VMEM_LIMIT_BYTES = 56 * 1024 * 1024
VMEM_LIMIT_LAYER_A_BYTES = 60 * 1024 * 1024


def _dot(a, b):
    return jnp.dot(a, b, preferred_element_type=F32)


def _dot_nt(a, b):
    return lax.dot_general(a, b, (((1,), (1,)), ((), ())), preferred_element_type=F32)


def _dot_tn(a, b):
    return lax.dot_general(a, b, (((0,), (0,)), ((), ())), preferred_element_type=F32)


def _rms(x):
    return x * lax.rsqrt(jnp.mean(x * x, axis=-1, keepdims=True) + EPS)


def _silu(g):
    return (0.5 * g) * (1.0 + jnp.tanh(0.5 * g))


def _mem_spec(arr, layer, nb):
    zeros = (0,) * (arr.ndim - 2)
    return pl.BlockSpec((None, nb) + arr.shape[2:], lambda b, t: (layer, b) + zeros)


def _const_spec(shape):
    nd = len(shape)
    return pl.BlockSpec(shape, lambda *_: (0,) * nd, pipeline_mode=pl.Buffered(1))


def _v_row(h, dt):
    return dt * N_HEADS + h


def _mem_attention(qm, gm, mk_ref, mv_ref, nb, rows, cat_scr, col0):
    row_view = len(mk_ref.shape) == 3
    for hh in range(N_HEADS):
        cs = slice(hh * MEM_HD, (hh + 1) * MEM_HD)
        if row_view:
            hsel = pl.ds(hh, mk_ref.shape[1] // N_HEADS, stride=N_HEADS)
            mk, mv = mk_ref[nb, hsel, :].astype(BF16), mv_ref[nb, hsel, :].astype(BF16)
        else:
            mk, mv = mk_ref[nb, hh], mv_ref[nb, hh]
        q = (qm[rows, cs] * (MEM_HD ** -0.5 * LOG2E)).astype(BF16)
        s = _dot_nt(q, mk)
        p = jnp.exp2(s - jnp.max(s, axis=-1, keepdims=True))
        l = jnp.sum(p, axis=-1, keepdims=True)
        o = _dot(p.astype(BF16), mv) / l
        cat_scr[rows, col0 + hh * MEM_HD: col0 + (hh + 1) * MEM_HD] = (o * _silu(gm[rows, cs])).astype(BF16)


def _memkv_kernel(mem_ref, g_ref, w_ref, mk_ref, mv_ref, mkb_ref, mvb_ref):
    tm = mem_ref.shape[0]
    base = _rms(mem_ref[...])
    nbm, M = mkb_ref.shape[1], mkb_ref.shape[3]
    for l in range(w_ref.shape[0]):
        kv = _dot((base * g_ref[l]).astype(BF16), w_ref[l])
        mk, mv = kv[:, :MEM_W], kv[:, MEM_W:]
        for hh in range(N_HEADS):
            cs = slice(hh * MEM_HD, (hh + 1) * MEM_HD)
            mk_ref[l, pl.ds(hh, tm, stride=N_HEADS), :] = mk[:, cs]
            mv_ref[l, pl.ds(hh, tm, stride=N_HEADS), :] = mv[:, cs]
            for bb in range(nbm):
                mkb_ref[l, bb, hh] = mk[bb * M:(bb + 1) * M, cs].astype(BF16)
                mvb_ref[l, bb, hh] = mv[bb * M:(bb + 1) * M, cs].astype(BF16)


def _memory_kv(mem, norm_mem, w_mem_kv):
    B, M, D = mem.shape
    nl = norm_mem.shape[0]
    rows = B * M
    tm = 512 if rows % 512 == 0 else M
    f = pl.pallas_call(
        _memkv_kernel,
        grid=(rows // tm,),
        in_specs=[
            pl.BlockSpec((tm, D), lambda i: (i, 0)),
            _const_spec((nl, 1, D)),
            _const_spec((nl, D, 2 * MEM_W)),
        ],
        out_specs=[pl.BlockSpec((nl, tm * N_HEADS, MEM_HD), lambda i: (0, i, 0))] * 2
        + [pl.BlockSpec((nl, tm // M, N_HEADS, M, MEM_HD), lambda i: (0, i, 0, 0, 0))] * 2,
        out_shape=[jax.ShapeDtypeStruct((nl, rows * N_HEADS, MEM_HD), F32)] * 2
        + [jax.ShapeDtypeStruct((nl, B, N_HEADS, M, MEM_HD), BF16)] * 2,
        compiler_params=pltpu.CompilerParams(
            dimension_semantics=("arbitrary",), vmem_limit_bytes=VMEM_LIMIT_BYTES),
        name="memkv",
    )
    mk, mv, mkb, mvb = f(mem.reshape(rows, D), norm_mem.reshape(nl, 1, D), w_mem_kv.astype(BF16))
    shp5 = (nl, B, M, N_HEADS, MEM_HD)
    return mk.reshape(shp5), mv.reshape(shp5), mkb, mvb


def _layer_a_kernel(*refs, NB, TT, L, has_state):
    it = iter(refs)
    x_ref, cos_ref, sin_ref = next(it), next(it), next(it)
    s0_ref = next(it) if has_state else None
    mk_ref, mv_ref, na_ref, win_ref, wout_ref, nkv_ref, wkv_ref = (next(it) for _ in range(7))
    h1_ref, k_ref, v_ref, kb_ref, vb_ref, S_ref = (next(it) for _ in range(6))
    h_scr, cat_scr, dec_scr = (next(it) for _ in range(3))

    R = NB * TT
    D = x_ref.shape[-1]
    QW = N_HEADS * HEAD_W
    t = pl.program_id(1)
    log_g = [math.log(1.0 - 2.0 ** (-5.0 - h)) for h in range(N_HEADS)]

    @pl.when((pl.program_id(0) == 0) & (t == 0))
    def _():
        d = (lax.broadcasted_iota(jnp.int32, (L, L), 0) - lax.broadcasted_iota(jnp.int32, (L, L), 1)).astype(F32)
        for h in range(N_HEADS):
            dec_scr[h] = jnp.where(d >= 0, jnp.exp(log_g[h] * jnp.maximum(d, 0.0)), 0.0)

    @pl.when(t == 0)
    def _():
        if has_state:
            S_ref[...] = s0_ref[...]
        else:
            S_ref[...] = jnp.zeros(S_ref.shape, F32)

    x = x_ref[...].reshape(R, D)
    h_scr[...] = (_rms(x) * na_ref[...]).astype(BF16)
    cos = cos_ref[...]
    sin = sin_ref[...]
    half = HEAD_W // 2
    idx = lax.broadcasted_iota(jnp.int32, (L, 1), 0).astype(F32)

    def rope(c0):
        y = _dot(h_scr[...], win_ref[:, c0:c0 + HEAD_W])
        y1, y2 = y[:, :half], y[:, half:]
        return jnp.concatenate([y1 * cos - y2 * sin, y2 * cos + y1 * sin], axis=-1)

    for h in range(N_HEADS):
        qf = rope(h * HEAD_W)
        q = qf.astype(BF16)
        k = rope(QW + h * HEAD_W) * (HEAD_W ** -0.5)
        v = _dot(h_scr[...], win_ref[:, 2 * QW + h * HEAD_W: 2 * QW + (h + 1) * HEAD_W]).astype(BF16)
        g = _dot(h_scr[...], win_ref[:, 3 * QW + h * HEAD_W: 3 * QW + (h + 1) * HEAD_W])
        a_in = jnp.exp(log_g[h] * (idx + 1.0))
        b_out = jnp.exp(log_g[h] * (L - 1.0 - idx))
        g_blk = math.exp(log_g[h] * L)
        for nb in range(NB):
            for j in range(TT // L):
                rows = slice(nb * TT + j * L, nb * TT + (j + 1) * L)
                qb, vb = q[rows], v[rows]
                kf = k[rows]
                S = S_ref[nb, h]
                sc = _dot_nt(qb, kf.astype(BF16)) * dec_scr[h]
                if L % LANES == 0:
                    lhs = jnp.concatenate([sc.astype(BF16), (qf[rows] * a_in).astype(BF16)], axis=1)
                    o = _dot(lhs, jnp.concatenate([vb, S.astype(BF16)], axis=0))
                else:
                    o = _dot(sc.astype(BF16), vb) + _dot(qb, S.astype(BF16)) * a_in
                S_ref[nb, h] = g_blk * S + _dot_tn((kf * b_out).astype(BF16), vb)
                cat_scr[rows, h * HEAD_W:(h + 1) * HEAD_W] = (_rms(o) * _silu(g[rows])).astype(BF16)

    qm = _dot(h_scr[...], win_ref[:, 4 * QW: 4 * QW + MEM_W])
    gm = _dot(h_scr[...], win_ref[:, 4 * QW + MEM_W: 4 * QW + 2 * MEM_W])
    for nb in range(NB):
        _mem_attention(qm, gm, mk_ref, mv_ref, nb, slice(nb * TT, (nb + 1) * TT), cat_scr, QW)

    NC = 512
    for c in range(D // NC):
        cs = slice(c * NC, (c + 1) * NC)
        y = _dot(cat_scr[...], wout_ref[:, cs]) + x_ref[:, :, cs].reshape(R, NC)
        h1_ref[:, :, cs] = y.reshape(NB, TT, NC)

    hk = (_rms(h1_ref[...].reshape(R, D)) * nkv_ref[...]).astype(BF16)
    KW = N_HEADS * HEAD_W
    for c in range(2 * KW // NC):
        kv = _dot(hk, wkv_ref[:, c * NC:(c + 1) * NC])
        is_k = c < KW // NC
        col0 = c * NC if is_k else c * NC - KW
        for hl in range(NC // HEAD_W):
            (kb_ref if is_k else vb_ref)[:, col0 // HEAD_W + hl] = (
                kv[:, hl * HEAD_W:(hl + 1) * HEAD_W].astype(BF16).reshape(NB, TT, HEAD_W))
        for jj in range(NC // LANES):
            col = col0 + jj * LANES
            if is_k:
                row = col // LANES
                k_ref[pl.ds(row, R, stride=KV_GROUPS), :] = kv[:, jj * LANES:(jj + 1) * LANES]
            else:
                row = _v_row(col // HEAD_W, (col % HEAD_W) // LANES)
                v_ref[pl.ds(row, R, stride=KV_GROUPS), :] = kv[:, jj * LANES:(jj + 1) * LANES]


def _layer_a(x, cos, sin, s0, mkb, mvb, norm_a, w_in, w_out, norm_kv, w_kv, *, NB, TT, L, mem_layer):
    B, T, D = x.shape
    R = NB * TT
    nt = T // TT
    assert NB == 1 or nt == 1
    KW = w_kv.shape[1] // 2
    has_state = s0 is not None
    bt = lambda b, t: (b, t, 0)
    b0 = lambda b, t: (b, 0, 0)
    in_specs = [pl.BlockSpec((NB, TT, D), bt),
                pl.BlockSpec((R, DQK), lambda b, t: (t, 0)),
                pl.BlockSpec((R, DQK), lambda b, t: (t, 0))]
    args = [x, cos, sin]
    if has_state:
        in_specs.append(pl.BlockSpec((NB, N_HEADS, HEAD_W, HEAD_W), lambda b, t: (b, 0, 0, 0)))
        args.append(s0)
    in_specs += [_mem_spec(mkb, mem_layer, NB), _mem_spec(mvb, mem_layer, NB),
                 _const_spec((1, D)), _const_spec(w_in.shape), _const_spec(w_out.shape),
                 _const_spec((1, D)), _const_spec(w_kv.shape)]
    args += [mkb, mvb, norm_a.reshape(1, D), w_in, w_out, norm_kv.reshape(1, D), w_kv]
    rows_spec = pl.BlockSpec((R * KV_GROUPS, LANES), lambda b, t: (b * nt + t, 0))
    assert KW == N_HEADS * HEAD_W
    hm_spec = pl.BlockSpec((NB, N_HEADS, TT, HEAD_W), lambda b, t: (b, 0, t, 0))
    out_specs = [pl.BlockSpec((NB, TT, D), bt), rows_spec, rows_spec, hm_spec, hm_spec,
                 pl.BlockSpec((NB, N_HEADS, HEAD_W, HEAD_W), lambda b, t: (b, 0, 0, 0))]
    out_shape = [jax.ShapeDtypeStruct((B, T, D), F32)] + [
        jax.ShapeDtypeStruct((B * T * KV_GROUPS, LANES), F32)] * 2 + [
        jax.ShapeDtypeStruct((B, N_HEADS, T, HEAD_W), BF16)] * 2 + [
        jax.ShapeDtypeStruct((B, N_HEADS, HEAD_W, HEAD_W), F32)]
    f = pl.pallas_call(
        functools.partial(_layer_a_kernel, NB=NB, TT=TT, L=L, has_state=has_state),
        grid=(B // NB, nt),
        in_specs=in_specs,
        out_specs=out_specs,
        out_shape=out_shape,
        scratch_shapes=[pltpu.VMEM((R, D), BF16), pltpu.VMEM((R, N_HEADS * HEAD_W + MEM_W), BF16),
                        pltpu.VMEM((N_HEADS, L, L), F32)],
        compiler_params=pltpu.CompilerParams(
            dimension_semantics=("arbitrary", "arbitrary"), vmem_limit_bytes=VMEM_LIMIT_LAYER_A_BYTES),
        name="layer_a",
    )
    return f(*args)


def _k_view_to_5d(kr, B, T):
    return kr.reshape(B, T, N_HEADS, 2, DQK)


def _v_view_to_4d(vr, B, T):
    return vr.reshape(B, T, 2, N_HEADS, LANES).transpose(0, 1, 3, 2, 4).reshape(B, T, N_HEADS, HEAD_W)


def _diff_lambda(lp_ref, lam_init):
    lp = lp_ref[...]
    a = jnp.sum(lp[0:1] * lp[1:2], axis=-1, keepdims=True)
    b = jnp.sum(lp[2:3] * lp[3:4], axis=-1, keepdims=True)
    return jnp.exp(a) - jnp.exp(b) + lam_init


def _diff_project_q(h1, nb_ref, win_ref, h_scr, q_scr):
    h_scr[...] = (_rms(h1) * nb_ref[...]).astype(BF16)
    for hh in range(N_HEADS):
        q = _dot(h_scr[...], win_ref[:, hh * HEAD_W:(hh + 1) * HEAD_W]) * (DQK ** -0.5 * LOG2E)
        q_scr[hh] = q.astype(BF16)


def _diff_finish(h1, o_of, lam_init, h_scr, win_ref, sub_ref, mk_ref, mv_ref, wout_ref, nf_ref, cat_scr, y_ref, NB=1):
    R, D = h1.shape
    T = R // NB
    QW = N_HEADS * HEAD_W
    for hh in range(N_HEADS):
        o = _rms(o_of(hh)) * sub_ref[...] * (1.0 - lam_init)
        g = _dot(h_scr[...], win_ref[:, QW + hh * HEAD_W: QW + (hh + 1) * HEAD_W])
        cat_scr[:, hh * HEAD_W:(hh + 1) * HEAD_W] = (o * _silu(g)).astype(BF16)
    qm = _dot(h_scr[...], win_ref[:, 2 * QW: 2 * QW + MEM_W])
    gm = _dot(h_scr[...], win_ref[:, 2 * QW + MEM_W: 2 * QW + 2 * MEM_W])
    for nb in range(NB):
        _mem_attention(qm, gm, mk_ref, mv_ref, nb, slice(nb * T, (nb + 1) * T), cat_scr, QW)
    NC = 512
    for c in range(D // NC):
        cs = slice(c * NC, (c + 1) * NC)
        y_ref[:, :, cs] = (_dot(cat_scr[...], wout_ref[:, cs]) + h1[:, cs]).reshape(NB, T, NC)
    y_ref[...] = (_rms(y_ref[...].reshape(R, D)) * nf_ref[...]).reshape(NB, T, D)


def _layer_b_prompt_kernel(h1_ref, kb_ref, vb_ref, mk_ref, mv_ref, nb_ref, win_ref, lp_ref, sub_ref, wout_ref,
                           nf_ref, y_ref, h_scr, q_scr, o_scr, cat_scr, *, TQ, SUB, NQ, lam_init):
    qi = pl.program_id(1)
    h1 = h1_ref[0]
    _diff_project_q(h1, nb_ref, win_ref, h_scr, q_scr)
    lam = _diff_lambda(lp_ref, lam_init)
    diag = (lax.broadcasted_iota(jnp.int32, (SUB, SUB), 1) // CHUNK) <= (
        lax.broadcasted_iota(jnp.int32, (SUB, SUB), 0) // CHUNK)

    for case in range(NQ):
        @pl.when(qi == case)
        def _(case=case):
            for sub in range(TQ // SUB):
                rows = slice(sub * SUB, (sub + 1) * SUB)
                kt = case * TQ + (sub + 1) * SUB
                for hh in range(N_HEADS):
                    ps, ls = [], []
                    qh = q_scr[hh, rows, :]
                    zq = jnp.zeros((SUB, DQK), BF16)
                    qs = jnp.concatenate([jnp.concatenate([qh[:, :DQK], zq], axis=1),
                                          jnp.concatenate([zq, qh[:, DQK:]], axis=1)], axis=0)
                    s_all = _dot_nt(qs, kb_ref[0, hh, :kt, :])
                    for c in range(2):
                        s = s_all[c * SUB:(c + 1) * SUB]
                        last = jnp.where(diag, s[:, kt - SUB:], NEG_BIG)
                        s = last if kt == SUB else jnp.concatenate([s[:, :kt - SUB], last], axis=1)
                        p = jnp.exp2(s - jnp.max(s, axis=-1, keepdims=True))
                        ls.append(jnp.sum(p, axis=-1, keepdims=True))
                        ps.append(p.astype(BF16))
                    pv = _dot(jnp.concatenate(ps, axis=0), vb_ref[0, hh, :kt, :])
                    o_scr[hh, rows] = pv[:SUB] / ls[0] - lam * (pv[SUB:] / ls[1])

    _diff_finish(h1, lambda hh: o_scr[hh], lam_init, h_scr, win_ref, sub_ref, mk_ref, mv_ref, wout_ref, nf_ref,
                 cat_scr, y_ref)


def _layer_b_prompt(h1, kb, vb, mkb, mvb, norm_b, w_in, lam_p, subln, w_out, norm_f, *, TQ, SUB, lam_init,
                    mem_layer):
    B, T, D = h1.shape
    kv_spec = pl.BlockSpec((1, N_HEADS, T, HEAD_W), lambda b, q: (b, 0, 0, 0))
    f = pl.pallas_call(
        functools.partial(_layer_b_prompt_kernel, TQ=TQ, SUB=SUB, NQ=T // TQ, lam_init=lam_init),
        grid=(B, T // TQ),
        in_specs=[pl.BlockSpec((1, TQ, D), lambda b, q: (b, q, 0)), kv_spec, kv_spec,
                  _mem_spec(mkb, mem_layer, 1), _mem_spec(mvb, mem_layer, 1),
                  _const_spec((1, D)), _const_spec(w_in.shape), _const_spec(lam_p.shape),
                  _const_spec((1, HEAD_W)), _const_spec(w_out.shape), _const_spec((1, D))],
        out_specs=pl.BlockSpec((1, TQ, D), lambda b, q: (b, q, 0)),
        out_shape=jax.ShapeDtypeStruct((B, T, D), F32),
        scratch_shapes=[pltpu.VMEM((TQ, D), BF16), pltpu.VMEM((N_HEADS, TQ, HEAD_W), BF16),
                        pltpu.VMEM((N_HEADS, TQ, HEAD_W), F32),
                        pltpu.VMEM((TQ, N_HEADS * HEAD_W + MEM_W), BF16)],
        compiler_params=pltpu.CompilerParams(
            dimension_semantics=("arbitrary", "arbitrary"), vmem_limit_bytes=VMEM_LIMIT_BYTES),
        name="layer_b_prompt",
    )
    return f(h1, kb, vb, mkb, mvb, norm_b.reshape(1, D), w_in, lam_p, subln.reshape(1, HEAD_W), w_out,
             norm_f.reshape(1, D))


def _online_tile(q_scr, k_of, v_of, m_scr, l_scr, acc_scr, rows):
    TQ = rows.stop - rows.start
    for hh in range(N_HEADS):
        ps, alphas = [], []
        for c in range(2):
            i = 2 * hh + c
            s = _dot_nt(q_scr[hh, rows, c * DQK:(c + 1) * DQK], k_of(hh, c))
            m_prev = m_scr[i, rows]
            m_new = jnp.maximum(m_prev, jnp.max(s, axis=-1, keepdims=True))
            alpha = jnp.exp2(m_prev - m_new)
            p = jnp.exp2(s - m_new)
            l_scr[i, rows] = alpha * l_scr[i, rows] + jnp.sum(p, axis=-1, keepdims=True)
            m_scr[i, rows] = m_new
            ps.append(p.astype(BF16))
            alphas.append(alpha)
        pv = _dot(jnp.concatenate(ps, axis=0), v_of(hh))
        acc_scr[2 * hh, rows] = alphas[0] * acc_scr[2 * hh, rows] + pv[:TQ]
        acc_scr[2 * hh + 1, rows] = alphas[1] * acc_scr[2 * hh + 1, rows] + pv[TQ:]


def _layer_b_sample_kernel(h1_ref, ck_ref, cv_ref, kn_ref, vn_ref, mk_ref, mv_ref, nb_ref, win_ref, lp_ref, sub_ref,
                           wout_ref, nf_ref, y_ref, h_scr, q_scr, m_scr, l_scr, acc_scr, cat_scr, *, TK, lam_init):
    kt = pl.program_id(1)
    NB, T, D = h1_ref.shape
    slot_rows = [slice(nb * T, (nb + 1) * T) for nb in range(NB)]

    @pl.when(kt == 0)
    def _():
        _diff_project_q(h1_ref[...].reshape(NB * T, D), nb_ref, win_ref, h_scr, q_scr)
        m_scr[...] = jnp.full(m_scr.shape, NEG_BIG, F32)
        l_scr[...] = jnp.zeros(l_scr.shape, F32)
        acc_scr[...] = jnp.zeros(acc_scr.shape, F32)

    for nb in range(NB):
        def ck_of(hh, c, nb=nb):
            return ck_ref[nb, pl.ds(2 * hh + c, TK, stride=KV_GROUPS), :].astype(BF16)

        def cv_of(hh, nb=nb):
            return jnp.concatenate([cv_ref[nb, pl.ds(_v_row(hh, dt), TK, stride=KV_GROUPS), :] for dt in range(2)],
                                   axis=1).astype(BF16)

        _online_tile(q_scr, ck_of, cv_of, m_scr, l_scr, acc_scr, slot_rows[nb])

    @pl.when(kt == pl.num_programs(1) - 1)
    def _():
        for nb in range(NB):
            _online_tile(q_scr,
                         lambda hh, c, nb=nb: kn_ref[nb, hh, :, c * DQK:(c + 1) * DQK],
                         lambda hh, nb=nb: vn_ref[nb, hh],
                         m_scr, l_scr, acc_scr, slot_rows[nb])
        lam = _diff_lambda(lp_ref, lam_init)
        o_of = lambda hh: (acc_scr[2 * hh] / l_scr[2 * hh]
                           - lam * (acc_scr[2 * hh + 1] / l_scr[2 * hh + 1]))
        _diff_finish(h1_ref[...].reshape(NB * T, D), o_of, lam_init, h_scr, win_ref, sub_ref, mk_ref, mv_ref,
                     wout_ref, nf_ref, cat_scr, y_ref, NB=NB)


def _layer_b_sample(h1, ck_rows, cv_rows, kb, vb, mkb, mvb, norm_b, w_in, lam_p, subln, w_out, norm_f, *, NB, TK,
                    lam_init, mem_layer):
    B, T, D = h1.shape
    R = NB * T
    nkt = ck_rows.shape[1] // (TK * KV_GROUPS)
    b0 = lambda b, k: (b, 0, 0)
    cache_spec = pl.BlockSpec((NB, TK * KV_GROUPS, LANES), lambda b, k: (b, k, 0))
    new_spec = pl.BlockSpec((NB, N_HEADS, T, HEAD_W), lambda b, k: (b, 0, 0, 0))
    f = pl.pallas_call(
        functools.partial(_layer_b_sample_kernel, TK=TK, lam_init=lam_init),
        grid=(B // NB, nkt),
        in_specs=[pl.BlockSpec((NB, T, D), b0), cache_spec, cache_spec, new_spec, new_spec,
                  _mem_spec(mkb, mem_layer, NB), _mem_spec(mvb, mem_layer, NB),
                  _const_spec((1, D)), _const_spec(w_in.shape), _const_spec(lam_p.shape),
                  _const_spec((1, HEAD_W)), _const_spec(w_out.shape), _const_spec((1, D))],
        out_specs=pl.BlockSpec((NB, T, D), b0),
        out_shape=jax.ShapeDtypeStruct((B, T, D), F32),
        scratch_shapes=[pltpu.VMEM((R, D), BF16), pltpu.VMEM((N_HEADS, R, HEAD_W), BF16),
                        pltpu.VMEM((2 * N_HEADS, R, 1), F32), pltpu.VMEM((2 * N_HEADS, R, 1), F32),
                        pltpu.VMEM((2 * N_HEADS, R, HEAD_W), F32),
                        pltpu.VMEM((R, N_HEADS * HEAD_W + MEM_W), BF16)],
        compiler_params=pltpu.CompilerParams(
            dimension_semantics=("arbitrary", "arbitrary"), vmem_limit_bytes=VMEM_LIMIT_BYTES),
        name="layer_b_sample",
    )
    return f(h1, ck_rows, cv_rows, kb, vb, mkb, mvb, norm_b.reshape(1, D), w_in, lam_p, subln.reshape(1, HEAD_W),
             w_out, norm_f.reshape(1, D))


def _rope_tables(pos):
    half = DQK
    inv = ROPE_BASE ** (-jnp.arange(half, dtype=F32) / half)
    ang = pos.astype(F32)[:, None] * inv[None, :]
    return jnp.cos(ang), jnp.sin(ang)


def _pick(n, candidates):
    for c in candidates:
        if n % c == 0:
            return c
    return n


def kernel(x_prompt, x_sample, state_ret, cache_k, cache_v, cache_mem_k, cache_mem_v, mem_prompt,
           norm_a, w_in_a, w_out_a, norm_kv, w_kv, norm_b, w_in_b, diff_lambda, subln_b, w_out_b,
           norm_mem, w_mem_kv, norm_f):
    depth = norm_mem.shape[0]
    assert norm_a.shape[0] == 1 and norm_b.shape[0] == 1 and depth == 2, "kernel is written for depth 2"
    lam_init = 0.8 - 0.6 * math.exp(-0.3 * 1)
    wia, woa, wkv = w_in_a[0].astype(BF16), w_out_a[0].astype(BF16), w_kv.astype(BF16)
    wib, wob = w_in_b[0].astype(BF16), w_out_b[0].astype(BF16)

    Bp, Tp, D = x_prompt.shape
    mk_p, mv_p, mkb_p, mvb_p = _memory_kv(mem_prompt, norm_mem, w_mem_kv)
    cos_p, sin_p = _rope_tables(jnp.arange(Tp))
    blk = _pick(Tp, (256, 128, CHUNK))
    tt = _pick(Tp, (2 * blk, blk))
    h1_p, k_p, v_p, kb_p, vb_p, S_p = _layer_a(
        x_prompt, cos_p, sin_p, None, mkb_p, mvb_p, norm_a[0], wia, woa, norm_kv, wkv, NB=1, TT=tt, L=tt,
        mem_layer=0)
    y_p = _layer_b_prompt(h1_p, kb_p, vb_p, mkb_p, mvb_p, norm_b[0], wib, diff_lambda[0], subln_b[0], wob,
                          norm_f, TQ=blk, SUB=blk, lam_init=lam_init, mem_layer=1)

    Bs, Ts, _ = x_sample.shape
    past = cache_k.shape[1]
    nbs = _pick(Bs, (4, 2, 1))
    cos_s, sin_s = _rope_tables(past + jnp.arange(Ts))
    cos_s, sin_s = jnp.tile(cos_s, (nbs, 1)), jnp.tile(sin_s, (nbs, 1))
    M = cache_mem_k.shape[2]
    cmk = cache_mem_k.reshape(depth, Bs, M * N_HEADS, MEM_HD)
    cmv = cache_mem_v.reshape(depth, Bs, M * N_HEADS, MEM_HD)
    h1_s, k_s, v_s, kb_s, vb_s, S_s = _layer_a(
        x_sample, cos_s, sin_s, state_ret[0], cmk, cmv, norm_a[0], wia, woa, norm_kv, wkv,
        NB=nbs, TT=Ts, L=Ts, mem_layer=0)
    ck_rows = cache_k.reshape(Bs, past * KV_GROUPS, LANES)
    cv_rows = cache_v.reshape(Bs, past, N_HEADS, 2, LANES).transpose(0, 1, 3, 2, 4).reshape(
        Bs, past * KV_GROUPS, LANES)
    nbb = _pick(Bs, (2, 1))
    y_s = _layer_b_sample(h1_s, ck_rows, cv_rows, kb_s, vb_s, cmk, cmv, norm_b[0], wib, diff_lambda[0],
                          subln_b[0], wob, norm_f, NB=nbb, TK=_pick(past, (2048 // nbb, 256, 128)), lam_init=lam_init,
                          mem_layer=1)

    return (y_p, y_s, S_p[None], _k_view_to_5d(k_p, Bp, Tp), _v_view_to_4d(v_p, Bp, Tp),
            mk_p, mv_p, S_s[None], _k_view_to_5d(k_s, Bs, Ts), _v_view_to_4d(v_s, Bs, Ts))
```
